```python
import jax, jax.numpy as jnp
from jax import lax
import numpy as np

D_MODEL = 1024
BATCH = 8
SEQ = 2048
DEPTH = 2

N_META = 16
D_FF = (11 * D_MODEL) // 4
LRU_WIDTH = D_MODEL // 4
LRU_BLOCKS = 4
LRU_BLOCK = LRU_WIDTH // LRU_BLOCKS
LRU_CONV = 4
LRU_C = 8.0
SC_WIDTH = D_MODEL // 4
SC_GROUPS = 4
SC_CONV = 3
RWKV_WIDTH = D_MODEL // 2
RWKV_HEAD = 64
RWKV_HEADS = RWKV_WIDTH // RWKV_HEAD
DECAY_LORA = 32
ICL_LORA = 32
GATE_LORA = 64
MIX_WIDTH = LRU_WIDTH + SC_WIDTH + RWKV_WIDTH
RWKV_IN = 3 * RWKV_WIDTH + DECAY_LORA + ICL_LORA + GATE_LORA
N_IN = 2 * LRU_WIDTH + 3 * SC_WIDTH + RWKV_IN
IN_SPLIT_IDX = (LRU_WIDTH, 2 * LRU_WIDTH, 2 * LRU_WIDTH + SC_WIDTH,
                2 * LRU_WIDTH + 2 * SC_WIDTH, 2 * LRU_WIDTH + 3 * SC_WIDTH)
RWKV_SPLIT_IDX = (RWKV_WIDTH, 2 * RWKV_WIDTH, 3 * RWKV_WIDTH,
                  3 * RWKV_WIDTH + DECAY_LORA, 3 * RWKV_WIDTH + DECAY_LORA + ICL_LORA)
RMS_EPS = 1e-6
LNX_EPS = 64e-5

kernel_name = "hybrid_rglru_shortconv_rwkv7_macaron"


def rms_norm(x, g):
    xf = x.astype(jnp.float32)
    y = xf * lax.rsqrt(jnp.mean(xf * xf, axis=-1, keepdims=True) + RMS_EPS)
    return (y * g).astype(x.dtype)


def group_rms_norm(x, g, n_groups):
    shp = x.shape
    xf = x.astype(jnp.float32).reshape(shp[:-1] + (n_groups, shp[-1] // n_groups))
    y = xf * lax.rsqrt(jnp.mean(xf * xf, axis=-1, keepdims=True) + RMS_EPS)
    return (y.reshape(shp) * g).astype(x.dtype)


def causal_dwconv(x, w):
    k_w, t = w.shape[0], x.shape[1]
    xp = jnp.pad(x, ((0, 0), (k_w - 1, 0), (0, 0)))
    y = xp[:, 0:t] * w[0]
    for k in range(1, k_w):
        y = y + xp[:, k:k + t] * w[k]
    return y


def token_shift(x):
    return jnp.pad(x, ((0, 0), (1, 0), (0, 0)))[:, :-1]


def swiglu(x, w_in, w_out):
    gate, up = jnp.split(x @ w_in, 2, axis=-1)
    return (jax.nn.silu(gate) * up) @ w_out


def _linear_rec_combine(c1, c2):
    a1, b1 = c1
    a2, b2 = c2
    return a1 * a2, a2 * b1 + b2


def rg_lru_mixer(xb, gb, conv_w, conv_b, wa, ba, wx, bx, lam):
    bsz, t, _ = xb.shape
    u = (causal_dwconv(xb, conv_w) + conv_b).astype(jnp.float32)
    ub = u.reshape(bsz, t, LRU_BLOCKS, LRU_BLOCK)
    r = jax.nn.sigmoid(jnp.einsum('btgi,gij->btgj', ub, wa).reshape(bsz, t, LRU_WIDTH) + ba)
    i = jax.nn.sigmoid(jnp.einsum('btgi,gij->btgj', ub, wx).reshape(bsz, t, LRU_WIDTH) + bx)
    log_a = -LRU_C * r * jax.nn.softplus(-lam)
    a = jnp.exp(log_a)
    b = jnp.sqrt(-jnp.expm1(2.0 * log_a)) * (i * u)
    _, h = lax.associative_scan(_linear_rec_combine, (a, b), axis=1)
    return jax.nn.gelu(gb.astype(jnp.float32)) * h


def short_conv_mixer(sc_b, sc_c, sc_x, conv_w):
    return sc_b * causal_dwconv(sc_c * sc_x, conv_w)


def rwkv7_mixer(z, mu, w0, w2, a0, a2, g2, k_k, k_a, r_k, lnx_w, lnx_b):
    bsz, t, _ = z.shape
    z = (z + (token_shift(z) - z) * mu).astype(jnp.float32)
    r, k, v, w_lo, a_lo, g_lo = jnp.split(z, RWKV_SPLIT_IDX, axis=-1)
    w_log = -jax.nn.softplus(-(w0 + jnp.tanh(w_lo) @ w2)) - 0.5
    decay = jnp.exp(-jnp.exp(w_log))
    a = jax.nn.sigmoid(a0 + a_lo @ a2)
    g = jax.nn.sigmoid(g_lo) @ g2
    kk = k * k_k
    k = k * (1.0 + (a - 1.0) * k_a)

    def heads(q):
        return q.reshape(bsz, t, RWKV_HEADS, RWKV_HEAD)

    r, k, v, kk, a, decay = (heads(q) for q in (r, k, v, kk, a, decay))
    kk = kk * lax.rsqrt(jnp.maximum(jnp.sum(kk * kk, axis=-1, keepdims=True), 1e-24))

    def step(s, inp):
        r_t, w_t, k_t, v_t, kk_t, a_t = inp
        sa = jnp.einsum('bhvk,bhk->bhv', s, -kk_t)
        s = (s * w_t[:, :, None, :] + sa[..., None] * (kk_t * a_t)[:, :, None, :]
             + v_t[..., None] * k_t[:, :, None, :])
        return s, jnp.einsum('bhvk,bhk->bhv', s, r_t)

    seq = tuple(jnp.moveaxis(q, 1, 0) for q in (r, decay, k, v, kk, a))
    s0 = jnp.zeros((bsz, RWKV_HEADS, RWKV_HEAD, RWKV_HEAD), jnp.float32)
    _, y = lax.scan(step, s0, seq)
    y = jnp.moveaxis(y, 0, 1)
    mean = jnp.mean(y, axis=-1, keepdims=True)
    var = jnp.mean(jnp.square(y - mean), axis=-1, keepdims=True)
    y = ((y - mean) * lax.rsqrt(var + LNX_EPS)).reshape(bsz, t, RWKV_WIDTH) * lnx_w + lnx_b
    bonus = jnp.sum(r * k * r_k.reshape(RWKV_HEADS, RWKV_HEAD), axis=-1, keepdims=True) * v
    y = y + bonus.reshape(bsz, t, RWKV_WIDTH)
    return y * g


def hybrid_mixer(u, w_in, w_out, lru_conv_w, lru_conv_b, lru_wa, lru_ba, lru_wx, lru_bx,
                 lru_lambda, lru_norm_g, sc_conv_w, sc_norm_g, rwkv_mu, rwkv_w0, rwkv_w2,
                 rwkv_a0, rwkv_a2, rwkv_g2, rwkv_k_k, rwkv_k_a, rwkv_r_k, rwkv_lnx_w, rwkv_lnx_b):
    p = u @ w_in
    lru_x, lru_g, sc_b, sc_c, sc_x, rw = jnp.split(p, IN_SPLIT_IDX, axis=-1)
    y_lru = group_rms_norm(
        rg_lru_mixer(lru_x, lru_g, lru_conv_w, lru_conv_b, lru_wa, lru_ba, lru_wx, lru_bx, lru_lambda),
        lru_norm_g, LRU_BLOCKS)
    y_sc = group_rms_norm(short_conv_mixer(sc_b, sc_c, sc_x, sc_conv_w), sc_norm_g, SC_GROUPS)
    y_rw = rwkv7_mixer(rw, rwkv_mu, rwkv_w0, rwkv_w2, rwkv_a0, rwkv_a2, rwkv_g2,
                       rwkv_k_k, rwkv_k_a, rwkv_r_k, rwkv_lnx_w, rwkv_lnx_b)
    y = jnp.concatenate([y_lru.astype(u.dtype), y_sc.astype(u.dtype), y_rw.astype(u.dtype)], axis=-1)
    return y @ w_out


def setup_inputs(seed: int = 0) -> dict:
    key = jax.random.key(seed)
    ks = jax.random.split(key, 32)
    L, D = DEPTH, D_MODEL
    nrm = jax.random.normal
    u01 = jax.random.uniform(ks[14], (L, LRU_WIDTH), minval=0.9, maxval=0.999)
    a_base = u01 ** (1.0 / LRU_C)
    return {
        'x': nrm(ks[0], (BATCH, SEQ, D), jnp.float32),
        'meta_tokens': nrm(ks[1], (N_META, D), jnp.float32),
        'norm_g': 1.0 + 0.02 * nrm(ks[2], (L, 6, D), jnp.float32),
        'ffn1_w_in': nrm(ks[3], (L, D, 2 * D_FF), jnp.float32) * D ** -0.5,
        'ffn1_w_out': nrm(ks[4], (L, D_FF, D), jnp.float32) * D_FF ** -0.5,
        'ffn2_w_in': nrm(ks[5], (L, D, 2 * D_FF), jnp.float32) * D ** -0.5,
        'ffn2_w_out': nrm(ks[6], (L, D_FF, D), jnp.float32) * D_FF ** -0.5,
        'mix_w_in': nrm(ks[7], (L, D, N_IN), jnp.float32) * D ** -0.5,
        'mix_w_out': nrm(ks[8], (L, MIX_WIDTH, D), jnp.float32) * MIX_WIDTH ** -0.5,
        'lru_conv_w': nrm(ks[9], (L, LRU_CONV, LRU_WIDTH), jnp.float32) * LRU_CONV ** -0.5,
        'lru_conv_b': 0.02 * nrm(ks[10], (L, LRU_WIDTH), jnp.float32),
        'lru_wa': nrm(ks[11], (L, LRU_BLOCKS, LRU_BLOCK, LRU_BLOCK), jnp.float32) * LRU_BLOCK ** -0.5,
        'lru_ba': 0.1 * nrm(ks[12], (L, LRU_WIDTH), jnp.float32),
        'lru_wx': nrm(ks[13], (L, LRU_BLOCKS, LRU_BLOCK, LRU_BLOCK), jnp.float32) * LRU_BLOCK ** -0.5,
        'lru_bx': 0.1 * nrm(ks[15], (L, LRU_WIDTH), jnp.float32),
        'lru_lambda': jnp.log(a_base) - jnp.log1p(-a_base),
        'lru_norm_g': 1.0 + 0.02 * nrm(ks[16], (L, LRU_WIDTH), jnp.float32),
        'sc_conv_w': nrm(ks[17], (L, SC_CONV, SC_WIDTH), jnp.float32) * SC_CONV ** -0.5,
        'sc_norm_g': 1.0 + 0.02 * nrm(ks[18], (L, SC_WIDTH), jnp.float32),
        'rwkv_mu': jax.random.uniform(ks[19], (L, RWKV_IN), jnp.float32),
        'rwkv_w0': jnp.linspace(-6.0, -1.0, RWKV_WIDTH, dtype=jnp.float32)[None, :]
                   + 0.1 * nrm(ks[20], (L, RWKV_WIDTH), jnp.float32),
        'rwkv_w2': nrm(ks[21], (L, DECAY_LORA, RWKV_WIDTH), jnp.float32) * DECAY_LORA ** -0.5,
        'rwkv_a0': 0.1 * nrm(ks[22], (L, RWKV_WIDTH), jnp.float32),
        'rwkv_a2': nrm(ks[23], (L, ICL_LORA, RWKV_WIDTH), jnp.float32) * ICL_LORA ** -0.5,
        'rwkv_g2': nrm(ks[24], (L, GATE_LORA, RWKV_WIDTH), jnp.float32) * GATE_LORA ** -0.5,
        'rwkv_k_k': 0.85 + 0.02 * nrm(ks[25], (L, RWKV_WIDTH), jnp.float32),
        'rwkv_k_a': 1.0 + 0.02 * nrm(ks[26], (L, RWKV_WIDTH), jnp.float32),
        'rwkv_r_k': 0.1 * nrm(ks[27], (L, RWKV_WIDTH), jnp.float32),
        'rwkv_lnx_w': 1.0 + 0.02 * nrm(ks[28], (L, RWKV_WIDTH), jnp.float32),
        'rwkv_lnx_b': 0.02 * nrm(ks[29], (L, RWKV_WIDTH), jnp.float32),
    }


def reference(x, meta_tokens, norm_g, ffn1_w_in, ffn1_w_out, ffn2_w_in, ffn2_w_out,
              mix_w_in, mix_w_out, lru_conv_w, lru_conv_b, lru_wa, lru_ba, lru_wx, lru_bx,
              lru_lambda, lru_norm_g, sc_conv_w, sc_norm_g, rwkv_mu, rwkv_w0, rwkv_w2,
              rwkv_a0, rwkv_a2, rwkv_g2, rwkv_k_k, rwkv_k_a, rwkv_r_k, rwkv_lnx_w, rwkv_lnx_b):
    bsz = x.shape[0]
    meta = jnp.broadcast_to(meta_tokens.astype(x.dtype)[None], (bsz, N_META, x.shape[-1]))
    h = jnp.concatenate([meta, x], axis=1)
    for l in range(DEPTH):
        g = norm_g[l]
        h = h + 0.5 * rms_norm(swiglu(rms_norm(h, g[0]), ffn1_w_in[l], ffn1_w_out[l]), g[1])
        m = hybrid_mixer(rms_norm(h, g[2]), mix_w_in[l], mix_w_out[l],
                         lru_conv_w[l], lru_conv_b[l], lru_wa[l], lru_ba[l], lru_wx[l], lru_bx[l],
                         lru_lambda[l], lru_norm_g[l], sc_conv_w[l], sc_norm_g[l],
                         rwkv_mu[l], rwkv_w0[l], rwkv_w2[l], rwkv_a0[l], rwkv_a2[l], rwkv_g2[l],
                         rwkv_k_k[l], rwkv_k_a[l], rwkv_r_k[l], rwkv_lnx_w[l], rwkv_lnx_b[l])
        h = h + rms_norm(m, g[3])
        h = h + 0.5 * rms_norm(swiglu(rms_norm(h, g[4]), ffn2_w_in[l], ffn2_w_out[l]), g[5])
    return h[:, N_META:]
```

```python
import functools

import jax
import jax.numpy as jnp
import numpy as np
from jax import lax
from jax.experimental import pallas as pl
from jax.experimental.pallas import tpu as pltpu

F32 = jnp.float32
BF16 = jnp.bfloat16

RMS_EPS = 1e-6
LNX_EPS = 64e-5
LRU_C = 8.0

HEAD = 64
CHUNK = 64
INV_BLOCK = 16
RWKV_ROWS = 704
LRU_ROWS = 192
FFN_ROWS = 1056
FFN_COLS = 256
PROJ_ROWS = 528
TIME_ALIGN = 2112
HALO = 8
VMEM_LIMIT = 56 * 1024 * 1024


def _cparams(n_axes):
    return pltpu.CompilerParams(
        dimension_semantics=("arbitrary",) * n_axes, vmem_limit_bytes=VMEM_LIMIT)


def _rms(x, g):
    return x * lax.rsqrt(jnp.mean(x * x, axis=-1, keepdims=True) + RMS_EPS) * g


def _mm(a, b):
    return jnp.dot(a.astype(BF16), b.astype(BF16), preferred_element_type=F32)


def _mm_nt(a, b):
    return lax.dot_general(a.astype(BF16), b.astype(BF16), (((1,), (1,)), ((), ())),
                           preferred_element_type=F32)


def _mm_tn(a, b):
    return lax.dot_general(a.astype(BF16), b.astype(BF16), (((0,), (0,)), ((), ())),
                           preferred_element_type=F32)


def _softplus(x):
    return jnp.maximum(x, 0.0) + jnp.log1p(jnp.exp(-jnp.abs(x)))


def _row_shift(x, s, fill):
    rows = lax.broadcasted_iota(jnp.int32, x.shape, 0)
    return jnp.where(rows >= s, pltpu.roll(x, s, axis=0), fill)


def _ffn_kernel(h_ref, gpre_ref, wg_ref, wu_ref, wo_ref, gpost_ref, o_ref, xn_ref, acc_ref):
    j = pl.program_id(1)

    @pl.when(j == 0)
    def _():
        xn_ref[...] = _rms(h_ref[...], gpre_ref[...]).astype(BF16)
        acc_ref[...] = jnp.zeros_like(acc_ref)

    xn = xn_ref[...]
    gate = jnp.dot(xn, wg_ref[...], preferred_element_type=F32)
    up = jnp.dot(xn, wu_ref[...], preferred_element_type=F32)
    act = (jax.nn.silu(gate) * up).astype(BF16)
    acc_ref[...] += jnp.dot(act, wo_ref[...], preferred_element_type=F32)

    @pl.when(j == pl.num_programs(1) - 1)
    def _():
        o_ref[...] = h_ref[...] + 0.5 * _rms(acc_ref[...], gpost_ref[...])


def _ffn(h, g_pre, w_in, w_out, g_post):
    n, d = h.shape
    d_ff = w_out.shape[0]
    nj = d_ff // FFN_COLS
    return pl.pallas_call(
        _ffn_kernel,
        grid=(n // FFN_ROWS, nj),
        in_specs=[
            pl.BlockSpec((FFN_ROWS, d), lambda i, j: (i, 0)),
            pl.BlockSpec((1, d), lambda i, j: (0, 0)),
            pl.BlockSpec((d, FFN_COLS), lambda i, j: (0, j)),
            pl.BlockSpec((d, FFN_COLS), lambda i, j: (0, j + nj)),
            pl.BlockSpec((FFN_COLS, d), lambda i, j: (j, 0)),
            pl.BlockSpec((1, d), lambda i, j: (0, 0)),
        ],
        out_specs=pl.BlockSpec((FFN_ROWS, d), lambda i, j: (i, 0)),
        out_shape=jax.ShapeDtypeStruct((n, d), F32),
        scratch_shapes=[pltpu.VMEM((FFN_ROWS, d), BF16), pltpu.VMEM((FFN_ROWS, d), F32)],
        compiler_params=_cparams(2),
        name="ffn",
    )(h, g_pre, w_in, w_in, w_out, g_post)


def _proj_in_kernel(h_ref, g_ref, w_ref, pa_ref, prw_ref):
    na = pa_ref.shape[1]
    xn = _rms(h_ref[...], g_ref[...]).astype(BF16)
    pa_ref[...] = jnp.dot(xn, w_ref[:, :na], preferred_element_type=F32)
    prw_ref[...] = jnp.dot(xn, w_ref[:, na:], preferred_element_type=F32)


def _proj_in(h, g, w, n_a):
    n, d = h.shape
    n_in = w.shape[1]
    return pl.pallas_call(
        _proj_in_kernel,
        grid=(n // PROJ_ROWS,),
        in_specs=[
            pl.BlockSpec((PROJ_ROWS, d), lambda i: (i, 0)),
            pl.BlockSpec((1, d), lambda i: (0, 0)),
            pl.BlockSpec((d, n_in), lambda i: (0, 0)),
        ],
        out_specs=[
            pl.BlockSpec((PROJ_ROWS, n_a), lambda i: (i, 0)),
            pl.BlockSpec((PROJ_ROWS, n_in - n_a), lambda i: (i, 0)),
        ],
        out_shape=[jax.ShapeDtypeStruct((n, n_a), F32),
                   jax.ShapeDtypeStruct((n, n_in - n_a), F32)],
        compiler_params=_cparams(1),
        name="proj_in",
    )(h, g, w)


def _proj_out_kernel(h_ref, ya_ref, yrw_ref, w_ref, g_ref, o_ref):
    na = ya_ref.shape[1]
    m = jnp.dot(ya_ref[...], w_ref[:na, :], preferred_element_type=F32)
    m = m + jnp.dot(yrw_ref[...], w_ref[na:, :], preferred_element_type=F32)
    o_ref[...] = h_ref[...] + _rms(m, g_ref[...])


def _proj_out(h, y_a, y_rw, w, g):
    n, d = h.shape
    na, nrw = y_a.shape[1], y_rw.shape[1]
    return pl.pallas_call(
        _proj_out_kernel,
        grid=(n // PROJ_ROWS,),
        in_specs=[
            pl.BlockSpec((PROJ_ROWS, d), lambda i: (i, 0)),
            pl.BlockSpec((PROJ_ROWS, na), lambda i: (i, 0)),
            pl.BlockSpec((PROJ_ROWS, nrw), lambda i: (i, 0)),
            pl.BlockSpec((na + nrw, d), lambda i: (0, 0)),
            pl.BlockSpec((1, d), lambda i: (0, 0)),
        ],
        out_specs=pl.BlockSpec((PROJ_ROWS, d), lambda i: (i, 0)),
        out_shape=jax.ShapeDtypeStruct((n, d), F32),
        compiler_params=_cparams(1),
        name="proj_out",
    )(h, y_a, y_rw, w, g)


def _group_rms(x, g, ones_bd):
    ms = _mm(x * x, ones_bd) * (1.0 / HEAD)
    return x * lax.rsqrt(ms + RMS_EPS) * g


def _lrusc_kernel(p_ref, cw_ref, cb_ref, wa_ref, ba_ref, wx_ref, bx_ref, lam_ref, lg_ref,
                  scw_ref, scg_ref, bd_ref, o_ref, xh_ref, ch_ref, hs_ref):
    w = cb_ref.shape[1]
    rows = p_ref.shape[0]

    @pl.when(pl.program_id(1) == 0)
    def _():
        xh_ref[...] = jnp.zeros_like(xh_ref)
        ch_ref[...] = jnp.zeros_like(ch_ref)
        hs_ref[...] = jnp.zeros_like(hs_ref)

    xb = p_ref[:, 0:w]
    gb = p_ref[:, w:2 * w]
    sc_b = p_ref[:, 2 * w:3 * w]
    cx = p_ref[:, 3 * w:4 * w] * p_ref[:, 4 * w:5 * w]
    ones_bd = bd_ref[...]

    xx = jnp.concatenate([xh_ref[...], xb], axis=0)
    u = xx[HALO - 3:HALO - 3 + rows] * cw_ref[0:1, :]
    for k in range(1, 4):
        u = u + xx[HALO - 3 + k:HALO - 3 + k + rows] * cw_ref[k:k + 1, :]
    u = u + cb_ref[...]
    xh_ref[...] = xb[rows - HALO:, :]

    r = jax.nn.sigmoid(_mm(u, wa_ref[...]) + ba_ref[...])
    gi = jax.nn.sigmoid(_mm(u, wx_ref[...]) + bx_ref[...])
    log_a = -LRU_C * r * _softplus(-lam_ref[...])
    a = jnp.exp(log_a)
    th = jnp.tanh(log_a)
    b = jnp.sqrt(-2.0 * th / (1.0 - th)) * (gi * u)

    s = 1
    while s < rows:
        b = a * _row_shift(b, s, 0.0) + b
        a = a * _row_shift(a, s, 1.0)
        s *= 2
    hseq = b + a * hs_ref[...]
    hs_ref[...] = hseq[rows - 1:rows, :]
    y_lru = _group_rms(jax.nn.gelu(gb) * hseq, lg_ref[...], ones_bd)

    cc = jnp.concatenate([ch_ref[...], cx], axis=0)
    conv = cc[HALO - 2:HALO - 2 + rows] * scw_ref[0:1, :]
    for k in range(1, 3):
        conv = conv + cc[HALO - 2 + k:HALO - 2 + k + rows] * scw_ref[k:k + 1, :]
    ch_ref[...] = cx[rows - HALO:, :]
    y_sc = _group_rms(sc_b * conv, scg_ref[...], ones_bd)

    o_ref[:, 0:w] = y_lru.astype(o_ref.dtype)
    o_ref[:, w:2 * w] = y_sc.astype(o_ref.dtype)


def _lrusc(p_a, params, batch, ones_bd):
    n, na = p_a.shape
    w = na // 5
    nt = n // batch // LRU_ROWS
    row = lambda b, t: (b * nt + t, 0)
    fixed = lambda b, t: (0, 0)
    in_specs = [pl.BlockSpec((LRU_ROWS, na), row)]
    in_specs += [pl.BlockSpec(p.shape, fixed) for p in params]
    in_specs += [pl.BlockSpec(ones_bd.shape, fixed)]
    return pl.pallas_call(
        _lrusc_kernel,
        grid=(batch, nt),
        in_specs=in_specs,
        out_specs=pl.BlockSpec((LRU_ROWS, 2 * w), row),
        out_shape=jax.ShapeDtypeStruct((n, 2 * w), BF16),
        scratch_shapes=[pltpu.VMEM((HALO, w), F32), pltpu.VMEM((HALO, w), F32),
                        pltpu.VMEM((1, w), F32)],
        compiler_params=_cparams(2),
        name="lru_shortconv",
    )(p_a, *params, ones_bd)


def _tri_inverse(lmat, same_blk, eye):
    ld = jnp.where(same_blk, lmat, 0.0)
    lo = lmat - ld
    t = eye + ld
    pw = ld
    s = 2
    while s < INV_BLOCK:
        pw = _mm(pw, pw)
        t = t + _mm(t, pw)
        s *= 2
    nb = CHUNK // INV_BLOCK
    nmat = _mm(t, lo)
    series = eye + nmat
    pw = nmat
    s = 2
    while s < nb:
        pw = _mm(pw, pw)
        series = series + _mm(series, pw)
        s *= 2
    return _mm(series, t)


def _rwkv_kernel(p_ref, mu_ref, w0_ref, a0_ref, wl_ref, kk_ref, ka_ref, rk_ref, lw_ref, lb_ref,
                 bd_ref, o_ref, zl_ref, s_ref, *, n_dl, n_il):
    width = w0_ref.shape[1]
    n_heads = width // HEAD
    n_chunks = p_ref.shape[0] // CHUNK

    @pl.when(pl.program_id(1) == 0)
    def _():
        zl_ref[...] = jnp.zeros_like(zl_ref)
        s_ref[...] = jnp.zeros_like(s_ref)

    c2 = 2 * CHUNK
    ri = lax.broadcasted_iota(jnp.int32, (c2, c2), 0)
    ci = lax.broadcasted_iota(jnp.int32, (c2, c2), 1)
    cm = jnp.bitwise_and(ci, CHUNK - 1)
    gmask = ((ri < CHUNK) & (cm < ri)) | ((ri >= CHUNK) & (cm <= ri - CHUNK))
    ti = lax.broadcasted_iota(jnp.int32, (CHUNK, CHUNK), 0)
    si = lax.broadcasted_iota(jnp.int32, (CHUNK, CHUNK), 1)
    shift = INV_BLOCK.bit_length() - 1
    same_blk = jnp.right_shift(ti, shift) == jnp.right_shift(si, shift)
    eye = (ti == si).astype(F32)
    tri = (si <= ti).astype(BF16)
    lane = lax.broadcasted_iota(jnp.int32, (CHUNK, wl_ref.shape[0]), 1)
    ones_bd = bd_ref[...]

    def seg_sum(x):
        return _mm(x, ones_bd)

    def chunk_body(c, carry):
        r0 = pl.multiple_of(c * CHUNK, CHUNK)
        z = p_ref[pl.ds(r0, CHUNK), :]
        zprev = jnp.where(lax.broadcasted_iota(jnp.int32, z.shape, 0) >= 1,
                          pltpu.roll(z, 1, axis=0), zl_ref[...])
        zl_ref[...] = z[CHUNK - 1:CHUNK, :]
        zz = z + (zprev - z) * mu_ref[...]
        r = zz[:, 0:width]
        k = zz[:, width:2 * width]
        v = zz[:, 2 * width:3 * width]
        lo = zz[:, 3 * width:]
        lo_act = jnp.where(lane < n_dl, jnp.tanh(lo),
                           jnp.where(lane < n_dl + n_il, lo, jax.nn.sigmoid(lo)))
        lora = _mm(lo_act, wl_ref[...])
        w_log = -_softplus(-(w0_ref[...] + lora[:, 0:width])) - 0.5
        lw = -jnp.exp(w_log)
        a = jax.nn.sigmoid(a0_ref[...] + lora[:, width:2 * width])
        g = lora[:, 2 * width:3 * width]
        kk = k * kk_ref[...]
        kf = k * (1.0 + (a - 1.0) * ka_ref[...])
        kk = kk * lax.rsqrt(jnp.maximum(seg_sum(kk * kk), 1e-24))
        bonus = seg_sum(r * kf * rk_ref[...])
        kka = kk * a

        lw_hi = lw.astype(BF16)
        lw_lo = (lw - lw_hi.astype(F32)).astype(BF16)
        cum = (jnp.dot(tri, lw_hi, preferred_element_type=F32)
               + jnp.dot(tri, lw_lo, preferred_element_type=F32))
        cl = cum[CHUNK - 1:CHUNK, :]
        w_in = jnp.exp(-cum)
        w_rem = jnp.exp(cl - cum)
        al = (-kk * jnp.exp(cum - lw)).astype(BF16)
        be = (kka * w_in).astype(BF16)
        kt = (kf * w_in).astype(BF16)
        rt = (r * jnp.exp(cum)).astype(BF16)
        bw = (kka * w_rem).astype(BF16)
        kw = (kf * w_rem).astype(BF16)
        vb = v.astype(BF16)
        w_c = jnp.exp(cl)

        ys = []
        for h in range(n_heads):
            hs = slice(h * HEAD, (h + 1) * HEAD)
            ar = jnp.concatenate([al[:, hs], rt[:, hs]], axis=0)
            bk = jnp.concatenate([be[:, hs], kt[:, hs]], axis=0)
            gm = jnp.where(gmask, _mm_nt(ar, bk), 0.0)
            tinv = _tri_inverse(gm[0:CHUNK, 0:CHUNK], same_blk, eye)
            s0 = s_ref[h]
            a_s = _mm_nt(ar, s0)
            v_h = vb[:, hs]
            rhs = a_s[0:CHUNK] + _mm(gm[0:CHUNK, CHUNK:], v_h)
            e = _mm(tinv, rhs)
            ev = jnp.concatenate([e.astype(BF16), v_h], axis=0)
            ys.append(a_s[CHUNK:] + _mm(gm[CHUNK:, :], ev))
            bkw = jnp.concatenate([bw[:, hs], kw[:, hs]], axis=0)
            s_ref[h] = s0 * w_c[:, hs] + _mm_tn(ev, bkw)
        y = jnp.concatenate(ys, axis=1)

        mean = seg_sum(y) * (1.0 / HEAD)
        d = y - mean
        var = seg_sum(d * d) * (1.0 / HEAD)
        yn = d * lax.rsqrt(var + LNX_EPS) * lw_ref[...] + lb_ref[...]
        o_ref[pl.ds(r0, CHUNK), :] = ((yn + bonus * v) * g).astype(o_ref.dtype)
        return carry

    lax.fori_loop(0, n_chunks, chunk_body, 0)


def _rwkv(p_rw, params, batch, ones_bd, n_dl, n_il):
    n, nrw = p_rw.shape
    width = params[1].shape[1]
    nt = n // batch // RWKV_ROWS
    row = lambda b, t: (b * nt + t, 0)
    fixed = lambda b, t: (0, 0)
    in_specs = [pl.BlockSpec((RWKV_ROWS, nrw), row)]
    in_specs += [pl.BlockSpec(p.shape, fixed) for p in params]
    in_specs += [pl.BlockSpec(ones_bd.shape, fixed)]
    return pl.pallas_call(
        functools.partial(_rwkv_kernel, n_dl=n_dl, n_il=n_il),
        grid=(batch, nt),
        in_specs=in_specs,
        out_specs=pl.BlockSpec((RWKV_ROWS, width), row),
        out_shape=jax.ShapeDtypeStruct((n, width), BF16),
        scratch_shapes=[pltpu.VMEM((1, nrw), F32),
                        pltpu.VMEM((width // HEAD, HEAD, HEAD), F32)],
        compiler_params=_cparams(2),
        name="rwkv7",
    )(p_rw, *params, ones_bd)


def _ones_block_diag(width):
    return jnp.asarray(np.kron(np.eye(width // HEAD), np.ones((HEAD, HEAD))), BF16)


def _block_diag(w):
    g, i, j = w.shape
    return jnp.einsum('gij,gh->gihj', w, jnp.eye(g, dtype=w.dtype)).reshape(g * i, g * j)


def kernel(x, meta_tokens, norm_g, ffn1_w_in, ffn1_w_out, ffn2_w_in, ffn2_w_out, mix_w_in, mix_w_out, lru_conv_w, lru_conv_b, lru_wa, lru_ba, lru_wx, lru_bx, lru_lambda, lru_norm_g, sc_conv_w, sc_norm_g, rwkv_mu, rwkv_w0, rwkv_w2, rwkv_a0, rwkv_a2, rwkv_g2, rwkv_k_k, rwkv_k_a, rwkv_r_k, rwkv_lnx_w, rwkv_lnx_b):
    bsz, seq, d = x.shape
    n_meta = meta_tokens.shape[0]
    depth = norm_g.shape[0]
    lru_w = lru_conv_b.shape[1]
    sc_w = sc_norm_g.shape[1]
    rw_w = rwkv_w0.shape[1]
    assert lru_w == sc_w
    n_a = 2 * lru_w + 3 * sc_w
    t_real = n_meta + seq
    t_pad = -(-t_real // TIME_ALIGN) * TIME_ALIGN

    meta = jnp.broadcast_to(meta_tokens.astype(x.dtype)[None], (bsz, n_meta, d))
    pad = jnp.zeros((bsz, t_pad - t_real, d), x.dtype)
    h = jnp.concatenate([meta, x, pad], axis=1).reshape(bsz * t_pad, d)

    bd_a = _ones_block_diag(lru_w)
    bd_rw = _ones_block_diag(rw_w)
    row = lambda p: p.reshape(1, -1)

    for l in range(depth):
        g = norm_g[l]
        h = _ffn(h, row(g[0]), ffn1_w_in[l].astype(BF16), ffn1_w_out[l].astype(BF16), row(g[1]))

        p_a, p_rw = _proj_in(h, row(g[2]), mix_w_in[l].astype(BF16), n_a)
        lru_params = (lru_conv_w[l], row(lru_conv_b[l]),
                      _block_diag(lru_wa[l]).astype(BF16), row(lru_ba[l]),
                      _block_diag(lru_wx[l]).astype(BF16), row(lru_bx[l]),
                      row(lru_lambda[l]), row(lru_norm_g[l]), sc_conv_w[l], row(sc_norm_g[l]))
        y_a = _lrusc(p_a, lru_params, bsz, bd_a)

        n_dl, n_il, n_gl = rwkv_w2.shape[1], rwkv_a2.shape[1], rwkv_g2.shape[1]
        lora_w = jnp.zeros((n_dl + n_il + n_gl, 3 * rw_w), F32)
        lora_w = lora_w.at[0:n_dl, 0:rw_w].set(rwkv_w2[l])
        lora_w = lora_w.at[n_dl:n_dl + n_il, rw_w:2 * rw_w].set(rwkv_a2[l])
        lora_w = lora_w.at[n_dl + n_il:, 2 * rw_w:].set(rwkv_g2[l])
        rw_params = (row(rwkv_mu[l]), row(rwkv_w0[l]), row(rwkv_a0[l]), lora_w.astype(BF16),
                     row(rwkv_k_k[l]), row(rwkv_k_a[l]), row(rwkv_r_k[l]),
                     row(rwkv_lnx_w[l]), row(rwkv_lnx_b[l]))
        y_rw = _rwkv(p_rw, rw_params, bsz, bd_rw, n_dl, n_il)

        h = _proj_out(h, y_a, y_rw, mix_w_out[l].astype(BF16), row(g[3]))
        h = _ffn(h, row(g[4]), ffn2_w_in[l].astype(BF16), ffn2_w_out[l].astype(BF16), row(g[5]))

    return h.reshape(bsz, t_pad, d)[:, n_meta:t_real]
```

```python
import functools

import jax
import jax.numpy as jnp
import numpy as np
from jax import lax
from jax.experimental import pallas as pl
from jax.experimental.pallas import tpu as pltpu

F32 = jnp.float32
BF16 = jnp.bfloat16

RMS_EPS = 1e-6
LNX_EPS = 64e-5
LRU_C = 8.0

HEAD = 64
CHUNK = 64
INV_BLOCK = 16
RWKV_ROWS = 704
LRU_ROWS = 192
FFN_ROWS = 1056
FFN_COLS = 256
PROJ_ROWS = 528
TIME_ALIGN = 2112
HALO = 8
VMEM_LIMIT = 56 * 1024 * 1024


def _cparams(n_axes):
    return pltpu.CompilerParams(
        dimension_semantics=("arbitrary",) * n_axes, vmem_limit_bytes=VMEM_LIMIT)


def _rms(x, g):
    return x * lax.rsqrt(jnp.mean(x * x, axis=-1, keepdims=True) + RMS_EPS) * g


def _mm(a, b):
    return jnp.dot(a.astype(BF16), b.astype(BF16), preferred_element_type=F32)


def _mm_nt(a, b):
    return lax.dot_general(a.astype(BF16), b.astype(BF16), (((1,), (1,)), ((), ())),
                           preferred_element_type=F32)


def _mm_tn(a, b):
    return lax.dot_general(a.astype(BF16), b.astype(BF16), (((0,), (0,)), ((), ())),
                           preferred_element_type=F32)


def _softplus(x):
    return jnp.maximum(x, 0.0) + jnp.log1p(jnp.exp(-jnp.abs(x)))


def _row_shift(x, s, fill):
    rows = lax.broadcasted_iota(jnp.int32, x.shape, 0)
    return jnp.where(rows >= s, pltpu.roll(x, s, axis=0), fill)


def _ffn_kernel(h_ref, gpre_ref, wg_ref, wu_ref, wo_ref, gpost_ref, o_ref, xn_ref, acc_ref):
    j = pl.program_id(1)

    @pl.when(j == 0)
    def _():
        xn_ref[...] = _rms(h_ref[...], gpre_ref[...]).astype(BF16)
        acc_ref[...] = jnp.zeros_like(acc_ref)

    xn = xn_ref[...]
    gate = jnp.dot(xn, wg_ref[...], preferred_element_type=F32)
    up = jnp.dot(xn, wu_ref[...], preferred_element_type=F32)
    act = (jax.nn.silu(gate) * up).astype(BF16)
    acc_ref[...] += jnp.dot(act, wo_ref[...], preferred_element_type=F32)

    @pl.when(j == pl.num_programs(1) - 1)
    def _():
        o_ref[...] = h_ref[...] + 0.5 * _rms(acc_ref[...], gpost_ref[...])


def _ffn(h, g_pre, w_in, w_out, g_post):
    n, d = h.shape
    d_ff = w_out.shape[0]
    nj = d_ff // FFN_COLS
    return pl.pallas_call(
        _ffn_kernel,
        grid=(n // FFN_ROWS, nj),
        in_specs=[
            pl.BlockSpec((FFN_ROWS, d), lambda i, j: (i, 0)),
            pl.BlockSpec((1, d), lambda i, j: (0, 0)),
            pl.BlockSpec((d, FFN_COLS), lambda i, j: (0, j)),
            pl.BlockSpec((d, FFN_COLS), lambda i, j: (0, j + nj)),
            pl.BlockSpec((FFN_COLS, d), lambda i, j: (j, 0)),
            pl.BlockSpec((1, d), lambda i, j: (0, 0)),
        ],
        out_specs=pl.BlockSpec((FFN_ROWS, d), lambda i, j: (i, 0)),
        out_shape=jax.ShapeDtypeStruct((n, d), F32),
        scratch_shapes=[pltpu.VMEM((FFN_ROWS, d), BF16), pltpu.VMEM((FFN_ROWS, d), F32)],
        compiler_params=_cparams(2),
        name="ffn",
    )(h, g_pre, w_in, w_in, w_out, g_post)


def _proj_in_kernel(h_ref, g_ref, w_ref, pa_ref, prw_ref):
    na = pa_ref.shape[1]
    xn = _rms(h_ref[...], g_ref[...]).astype(BF16)
    pa_ref[...] = jnp.dot(xn, w_ref[:, :na], preferred_element_type=F32)
    prw_ref[...] = jnp.dot(xn, w_ref[:, na:], preferred_element_type=F32)


def _proj_in(h, g, w, n_a):
    n, d = h.shape
    n_in = w.shape[1]
    return pl.pallas_call(
        _proj_in_kernel,
        grid=(n // PROJ_ROWS,),
        in_specs=[
            pl.BlockSpec((PROJ_ROWS, d), lambda i: (i, 0)),
            pl.BlockSpec((1, d), lambda i: (0, 0)),
            pl.BlockSpec((d, n_in), lambda i: (0, 0)),
        ],
        out_specs=[
            pl.BlockSpec((PROJ_ROWS, n_a), lambda i: (i, 0)),
            pl.BlockSpec((PROJ_ROWS, n_in - n_a), lambda i: (i, 0)),
        ],
        out_shape=[jax.ShapeDtypeStruct((n, n_a), F32),
                   jax.ShapeDtypeStruct((n, n_in - n_a), F32)],
        compiler_params=_cparams(1),
        name="proj_in",
    )(h, g, w)


def _proj_out_kernel(h_ref, ya_ref, yrw_ref, w_ref, g_ref, o_ref):
    na = ya_ref.shape[1]
    m = jnp.dot(ya_ref[...], w_ref[:na, :], preferred_element_type=F32)
    m = m + jnp.dot(yrw_ref[...], w_ref[na:, :], preferred_element_type=F32)
    o_ref[...] = h_ref[...] + _rms(m, g_ref[...])


def _proj_out(h, y_a, y_rw, w, g):
    n, d = h.shape
    na, nrw = y_a.shape[1], y_rw.shape[1]
    return pl.pallas_call(
        _proj_out_kernel,
        grid=(n // PROJ_ROWS,),
        in_specs=[
            pl.BlockSpec((PROJ_ROWS, d), lambda i: (i, 0)),
            pl.BlockSpec((PROJ_ROWS, na), lambda i: (i, 0)),
            pl.BlockSpec((PROJ_ROWS, nrw), lambda i: (i, 0)),
            pl.BlockSpec((na + nrw, d), lambda i: (0, 0)),
            pl.BlockSpec((1, d), lambda i: (0, 0)),
        ],
        out_specs=pl.BlockSpec((PROJ_ROWS, d), lambda i: (i, 0)),
        out_shape=jax.ShapeDtypeStruct((n, d), F32),
        compiler_params=_cparams(1),
        name="proj_out",
    )(h, y_a, y_rw, w, g)


def _group_rms(x, g, ones_bd):
    ms = _mm(x * x, ones_bd) * (1.0 / HEAD)
    return x * lax.rsqrt(ms + RMS_EPS) * g


def _lrusc_kernel(p_ref, cw_ref, cb_ref, wa_ref, ba_ref, wx_ref, bx_ref, lam_ref, lg_ref,
                  scw_ref, scg_ref, bd_ref, o_ref, xh_ref, ch_ref, hs_ref):
    w = cb_ref.shape[1]
    rows = p_ref.shape[0]

    @pl.when(pl.program_id(1) == 0)
    def _():
        xh_ref[...] = jnp.zeros_like(xh_ref)
        ch_ref[...] = jnp.zeros_like(ch_ref)
        hs_ref[...] = jnp.zeros_like(hs_ref)

    xb = p_ref[:, 0:w]
    gb = p_ref[:, w:2 * w]
    sc_b = p_ref[:, 2 * w:3 * w]
    cx = p_ref[:, 3 * w:4 * w] * p_ref[:, 4 * w:5 * w]
    ones_bd = bd_ref[...]

    xx = jnp.concatenate([xh_ref[...], xb], axis=0)
    u = xx[HALO - 3:HALO - 3 + rows] * cw_ref[0:1, :]
    for k in range(1, 4):
        u = u + xx[HALO - 3 + k:HALO - 3 + k + rows] * cw_ref[k:k + 1, :]
    u = u + cb_ref[...]
    xh_ref[...] = xb[rows - HALO:, :]

    r = jax.nn.sigmoid(_mm(u, wa_ref[...]) + ba_ref[...])
    gi = jax.nn.sigmoid(_mm(u, wx_ref[...]) + bx_ref[...])
    log_a = -LRU_C * r * _softplus(-lam_ref[...])
    a = jnp.exp(log_a)
    th = jnp.tanh(log_a)
    b = jnp.sqrt(-2.0 * th / (1.0 - th)) * (gi * u)

    s = 1
    while s < rows:
        b = a * _row_shift(b, s, 0.0) + b
        a = a * _row_shift(a, s, 1.0)
        s *= 2
    hseq = b + a * hs_ref[...]
    hs_ref[...] = hseq[rows - 1:rows, :]
    y_lru = _group_rms(jax.nn.gelu(gb) * hseq, lg_ref[...], ones_bd)

    cc = jnp.concatenate([ch_ref[...], cx], axis=0)
    conv = cc[HALO - 2:HALO - 2 + rows] * scw_ref[0:1, :]
    for k in range(1, 3):
        conv = conv + cc[HALO - 2 + k:HALO - 2 + k + rows] * scw_ref[k:k + 1, :]
    ch_ref[...] = cx[rows - HALO:, :]
    y_sc = _group_rms(sc_b * conv, scg_ref[...], ones_bd)

    o_ref[:, 0:w] = y_lru.astype(o_ref.dtype)
    o_ref[:, w:2 * w] = y_sc.astype(o_ref.dtype)


def _lrusc(p_a, params, batch, ones_bd):
    n, na = p_a.shape
    w = na // 5
    nt = n // batch // LRU_ROWS
    row = lambda b, t: (b * nt + t, 0)
    fixed = lambda b, t: (0, 0)
    in_specs = [pl.BlockSpec((LRU_ROWS, na), row)]
    in_specs += [pl.BlockSpec(p.shape, fixed) for p in params]
    in_specs += [pl.BlockSpec(ones_bd.shape, fixed)]
    return pl.pallas_call(
        _lrusc_kernel,
        grid=(batch, nt),
        in_specs=in_specs,
        out_specs=pl.BlockSpec((LRU_ROWS, 2 * w), row),
        out_shape=jax.ShapeDtypeStruct((n, 2 * w), BF16),
        scratch_shapes=[pltpu.VMEM((HALO, w), F32), pltpu.VMEM((HALO, w), F32),
                        pltpu.VMEM((1, w), F32)],
        compiler_params=_cparams(2),
        name="lru_shortconv",
    )(p_a, *params, ones_bd)


def _solve_unit_lower(lmats, rhss, same_blk, eye):
    lds = [jnp.where(same_blk, l, 0.0) for l in lmats]
    los = [l - ld for l, ld in zip(lmats, lds)]
    ts = [eye + ld for ld in lds]
    pws = lds
    s = 2
    while s < INV_BLOCK:
        pws = [_mm(pw, pw) for pw in pws]
        ts = [t + _mm(t, pw) for t, pw in zip(ts, pws)]
        s *= 2
    xs = [_mm(t, r) for t, r in zip(ts, rhss)]
    pws = [_mm(t, lo) for t, lo in zip(ts, los)]
    nb = CHUNK // INV_BLOCK
    s = 1
    while s < nb:
        xs = [x + _mm(pw, x) for x, pw in zip(xs, pws)]
        s *= 2
        if s < nb:
            pws = [_mm(pw, pw) for pw in pws]
    return xs


def _rwkv_kernel(p_ref, mu_ref, w0_ref, a0_ref, wl_ref, kk_ref, ka_ref, rk_ref, lw_ref, lb_ref,
                 bd_ref, o_ref, zl_ref, s_ref, *, n_dl, n_il):
    width = w0_ref.shape[1]
    n_heads = width // HEAD
    n_chunks = p_ref.shape[0] // CHUNK

    @pl.when(pl.program_id(1) == 0)
    def _():
        zl_ref[...] = jnp.zeros_like(zl_ref)
        s_ref[...] = jnp.zeros_like(s_ref)

    c2 = 2 * CHUNK
    ri = lax.broadcasted_iota(jnp.int32, (c2, c2), 0)
    ci = lax.broadcasted_iota(jnp.int32, (c2, c2), 1)
    cm = jnp.bitwise_and(ci, CHUNK - 1)
    gmask = ((ri < CHUNK) & (cm < ri)) | ((ri >= CHUNK) & (cm <= ri - CHUNK))
    ti = lax.broadcasted_iota(jnp.int32, (CHUNK, CHUNK), 0)
    si = lax.broadcasted_iota(jnp.int32, (CHUNK, CHUNK), 1)
    shift = INV_BLOCK.bit_length() - 1
    same_blk = jnp.right_shift(ti, shift) == jnp.right_shift(si, shift)
    eye = (ti == si).astype(F32)
    tri = (si <= ti).astype(BF16)
    lane = lax.broadcasted_iota(jnp.int32, (CHUNK, wl_ref.shape[0]), 1)
    ones_bd = bd_ref[...]

    def seg_sum(x):
        return _mm(x, ones_bd)

    def chunk_body(c, carry):
        r0 = pl.multiple_of(c * CHUNK, CHUNK)
        z = p_ref[pl.ds(r0, CHUNK), :]
        zprev = jnp.where(lax.broadcasted_iota(jnp.int32, z.shape, 0) >= 1,
                          pltpu.roll(z, 1, axis=0), zl_ref[...])
        zl_ref[...] = z[CHUNK - 1:CHUNK, :]
        zz = z + (zprev - z) * mu_ref[...]
        r = zz[:, 0:width]
        k = zz[:, width:2 * width]
        v = zz[:, 2 * width:3 * width]
        lo = zz[:, 3 * width:]
        lo_act = jnp.where(lane < n_dl, jnp.tanh(lo),
                           jnp.where(lane < n_dl + n_il, lo, jax.nn.sigmoid(lo)))
        lora = _mm(lo_act, wl_ref[...])
        w_log = -_softplus(-(w0_ref[...] + lora[:, 0:width])) - 0.5
        lw = -jnp.exp(w_log)
        a = jax.nn.sigmoid(a0_ref[...] + lora[:, width:2 * width])
        g = lora[:, 2 * width:3 * width]
        kk = k * kk_ref[...]
        kf = k * (1.0 + (a - 1.0) * ka_ref[...])
        kk = kk * lax.rsqrt(jnp.maximum(seg_sum(kk * kk), 1e-24))
        bonus = seg_sum(r * kf * rk_ref[...])
        kka = kk * a

        lw_hi = lw.astype(BF16)
        lw_lo = (lw - lw_hi.astype(F32)).astype(BF16)
        cum = (jnp.dot(tri, lw_hi, preferred_element_type=F32)
               + jnp.dot(tri, lw_lo, preferred_element_type=F32))
        cl = cum[CHUNK - 1:CHUNK, :]
        w_in = jnp.exp(-cum)
        w_rem = jnp.exp(cl - cum)
        al = (-kk * jnp.exp(cum - lw)).astype(BF16)
        be = (kka * w_in).astype(BF16)
        kt = (kf * w_in).astype(BF16)
        rt = (r * jnp.exp(cum)).astype(BF16)
        bw = (kka * w_rem).astype(BF16)
        kw = (kf * w_rem).astype(BF16)
        vb = v.astype(BF16)
        w_c = jnp.exp(cl)

        heads = range(n_heads)
        hsl = [slice(h * HEAD, (h + 1) * HEAD) for h in heads]
        ars = [jnp.concatenate([al[:, hs], rt[:, hs]], axis=0) for hs in hsl]
        bks = [jnp.concatenate([be[:, hs], kt[:, hs]], axis=0) for hs in hsl]
        bkws = [jnp.concatenate([bw[:, hs], kw[:, hs]], axis=0) for hs in hsl]
        vhs = [vb[:, hs] for hs in hsl]
        s0s = [s_ref[h] for h in heads]
        gms = [jnp.where(gmask, _mm_nt(ar, bk), 0.0) for ar, bk in zip(ars, bks)]
        a_ss = [_mm_nt(ar, s0) for ar, s0 in zip(ars, s0s)]
        akvs = [_mm(gm[0:CHUNK, CHUNK:], v_h) for gm, v_h in zip(gms, vhs)]
        rhss = [a_s[0:CHUNK] + akv for a_s, akv in zip(a_ss, akvs)]
        es = _solve_unit_lower([gm[0:CHUNK, 0:CHUNK] for gm in gms], rhss, same_blk, eye)
        evs = [jnp.concatenate([e.astype(BF16), v_h], axis=0) for e, v_h in zip(es, vhs)]
        ys = [a_s[CHUNK:] + _mm(gm[CHUNK:, :], ev) for a_s, gm, ev in zip(a_ss, gms, evs)]
        for h in heads:
            s_ref[h] = s0s[h] * w_c[:, hsl[h]] + _mm_tn(evs[h], bkws[h])
        y = jnp.concatenate(ys, axis=1)

        mean = seg_sum(y) * (1.0 / HEAD)
        d = y - mean
        var = seg_sum(d * d) * (1.0 / HEAD)
        yn = d * lax.rsqrt(var + LNX_EPS) * lw_ref[...] + lb_ref[...]
        o_ref[pl.ds(r0, CHUNK), :] = ((yn + bonus * v) * g).astype(o_ref.dtype)
        return carry

    lax.fori_loop(0, n_chunks, chunk_body, 0)


def _rwkv(p_rw, params, batch, ones_bd, n_dl, n_il):
    n, nrw = p_rw.shape
    width = params[1].shape[1]
    nt = n // batch // RWKV_ROWS
    row = lambda b, t: (b * nt + t, 0)
    fixed = lambda b, t: (0, 0)
    in_specs = [pl.BlockSpec((RWKV_ROWS, nrw), row)]
    in_specs += [pl.BlockSpec(p.shape, fixed) for p in params]
    in_specs += [pl.BlockSpec(ones_bd.shape, fixed)]
    return pl.pallas_call(
        functools.partial(_rwkv_kernel, n_dl=n_dl, n_il=n_il),
        grid=(batch, nt),
        in_specs=in_specs,
        out_specs=pl.BlockSpec((RWKV_ROWS, width), row),
        out_shape=jax.ShapeDtypeStruct((n, width), BF16),
        scratch_shapes=[pltpu.VMEM((1, nrw), F32),
                        pltpu.VMEM((width // HEAD, HEAD, HEAD), F32)],
        compiler_params=_cparams(2),
        name="rwkv7",
    )(p_rw, *params, ones_bd)


def _ones_block_diag(width):
    return jnp.asarray(np.kron(np.eye(width // HEAD), np.ones((HEAD, HEAD))), BF16)


def _block_diag(w):
    g, i, j = w.shape
    return jnp.einsum('gij,gh->gihj', w, jnp.eye(g, dtype=w.dtype)).reshape(g * i, g * j)


def kernel(x, meta_tokens, norm_g, ffn1_w_in, ffn1_w_out, ffn2_w_in, ffn2_w_out, mix_w_in, mix_w_out, lru_conv_w, lru_conv_b, lru_wa, lru_ba, lru_wx, lru_bx, lru_lambda, lru_norm_g, sc_conv_w, sc_norm_g, rwkv_mu, rwkv_w0, rwkv_w2, rwkv_a0, rwkv_a2, rwkv_g2, rwkv_k_k, rwkv_k_a, rwkv_r_k, rwkv_lnx_w, rwkv_lnx_b):
    bsz, seq, d = x.shape
    n_meta = meta_tokens.shape[0]
    depth = norm_g.shape[0]
    lru_w = lru_conv_b.shape[1]
    sc_w = sc_norm_g.shape[1]
    rw_w = rwkv_w0.shape[1]
    assert lru_w == sc_w
    n_a = 2 * lru_w + 3 * sc_w
    t_real = n_meta + seq
    t_pad = -(-t_real // TIME_ALIGN) * TIME_ALIGN

    meta = jnp.broadcast_to(meta_tokens.astype(x.dtype)[None], (bsz, n_meta, d))
    pad = jnp.zeros((bsz, t_pad - t_real, d), x.dtype)
    h = jnp.concatenate([meta, x, pad], axis=1).reshape(bsz * t_pad, d)

    bd_a = _ones_block_diag(lru_w)
    bd_rw = _ones_block_diag(rw_w)
    row = lambda p: p.reshape(1, -1)

    for l in range(depth):
        g = norm_g[l]
        h = _ffn(h, row(g[0]), ffn1_w_in[l].astype(BF16), ffn1_w_out[l].astype(BF16), row(g[1]))

        p_a, p_rw = _proj_in(h, row(g[2]), mix_w_in[l].astype(BF16), n_a)
        lru_params = (lru_conv_w[l], row(lru_conv_b[l]),
                      _block_diag(lru_wa[l]).astype(BF16), row(lru_ba[l]),
                      _block_diag(lru_wx[l]).astype(BF16), row(lru_bx[l]),
                      row(lru_lambda[l]), row(lru_norm_g[l]), sc_conv_w[l], row(sc_norm_g[l]))
        y_a = _lrusc(p_a, lru_params, bsz, bd_a)

        n_dl, n_il, n_gl = rwkv_w2.shape[1], rwkv_a2.shape[1], rwkv_g2.shape[1]
        lora_w = jnp.zeros((n_dl + n_il + n_gl, 3 * rw_w), F32)
        lora_w = lora_w.at[0:n_dl, 0:rw_w].set(rwkv_w2[l])
        lora_w = lora_w.at[n_dl:n_dl + n_il, rw_w:2 * rw_w].set(rwkv_a2[l])
        lora_w = lora_w.at[n_dl + n_il:, 2 * rw_w:].set(rwkv_g2[l])
        rw_params = (row(rwkv_mu[l]), row(rwkv_w0[l]), row(rwkv_a0[l]), lora_w.astype(BF16),
                     row(rwkv_k_k[l]), row(rwkv_k_a[l]), row(rwkv_r_k[l]),
                     row(rwkv_lnx_w[l]), row(rwkv_lnx_b[l]))
        y_rw = _rwkv(p_rw, rw_params, bsz, bd_rw, n_dl, n_il)

        h = _proj_out(h, y_a, y_rw, mix_w_out[l].astype(BF16), row(g[3]))
        h = _ffn(h, row(g[4]), ffn2_w_in[l].astype(BF16), ffn2_w_out[l].astype(BF16), row(g[5]))

    return h.reshape(bsz, t_pad, d)[:, n_meta:t_real]
```

```python
import functools

import jax
import jax.numpy as jnp
import numpy as np
from jax import lax
from jax.experimental import pallas as pl
from jax.experimental.pallas import tpu as pltpu

F32 = jnp.float32
BF16 = jnp.bfloat16

RMS_EPS = 1e-6
LNX_EPS = 64e-5
LRU_C = 8.0

HEAD = 64
CHUNK = 64
INV_BLOCK = 16
RWKV_ROWS = 192
RWKV_SLOTS = 4
LRU_ROWS = 192
FFN_ROWS = 528
PROJ_ROWS = 528
TIME_ALIGN = 2112
HALO = 8
VMEM_LIMIT = 56 * 1024 * 1024


def _cparams(n_axes):
    return pltpu.CompilerParams(
        dimension_semantics=("arbitrary",) * n_axes, vmem_limit_bytes=VMEM_LIMIT)


def _rms(x, g):
    return x * lax.rsqrt(jnp.mean(x * x, axis=-1, keepdims=True) + RMS_EPS) * g


def _mm(a, b):
    return jnp.dot(a.astype(BF16), b.astype(BF16), preferred_element_type=F32)


def _mm_nt(a, b):
    return lax.dot_general(a.astype(BF16), b.astype(BF16), (((1,), (1,)), ((), ())),
                           preferred_element_type=F32)


def _mm_tn(a, b):
    return lax.dot_general(a.astype(BF16), b.astype(BF16), (((0,), (0,)), ((), ())),
                           preferred_element_type=F32)


def _softplus(x):
    return jnp.maximum(x, 0.0) + jnp.log1p(jnp.exp(-jnp.abs(x)))


def _row_shift(x, s, fill):
    rows = lax.broadcasted_iota(jnp.int32, x.shape, 0)
    return jnp.where(rows >= s, pltpu.roll(x, s, axis=0), fill)


def _ffn_kernel(h_ref, gpre_ref, wi_ref, wo_ref, gpost_ref, o_ref):
    d_ff = wo_ref.shape[0]
    h = h_ref[...]
    xn = _rms(h, gpre_ref[...]).astype(BF16)
    gu = jnp.dot(xn, wi_ref[...], preferred_element_type=F32)
    act = (jax.nn.silu(gu[:, :d_ff]) * gu[:, d_ff:]).astype(BF16)
    y = jnp.dot(act, wo_ref[...], preferred_element_type=F32)
    o_ref[...] = h + 0.5 * _rms(y, gpost_ref[...])


def _ffn(h, g_pre, w_in, w_out, g_post):
    n, d = h.shape
    d_ff = w_out.shape[0]
    resident = lambda shape: pl.BlockSpec(shape, lambda i: (0, 0), pipeline_mode=pl.Buffered(1))
    return pl.pallas_call(
        _ffn_kernel,
        grid=(n // FFN_ROWS,),
        in_specs=[
            pl.BlockSpec((FFN_ROWS, d), lambda i: (i, 0)),
            resident((1, d)),
            resident((d, 2 * d_ff)),
            resident((d_ff, d)),
            resident((1, d)),
        ],
        out_specs=pl.BlockSpec((FFN_ROWS, d), lambda i: (i, 0)),
        out_shape=jax.ShapeDtypeStruct((n, d), F32),
        compiler_params=_cparams(1),
        name="ffn",
    )(h, g_pre, w_in, w_out, g_post)


def _proj_in_kernel(h_ref, g_ref, w_ref, pa_ref, prw_ref):
    na = pa_ref.shape[1]
    xn = _rms(h_ref[...], g_ref[...]).astype(BF16)
    pa_ref[...] = jnp.dot(xn, w_ref[:, :na], preferred_element_type=F32)
    prw_ref[...] = jnp.dot(xn, w_ref[:, na:], preferred_element_type=F32)


def _proj_in(h, g, w, n_a):
    n, d = h.shape
    n_in = w.shape[1]
    return pl.pallas_call(
        _proj_in_kernel,
        grid=(n // PROJ_ROWS,),
        in_specs=[
            pl.BlockSpec((PROJ_ROWS, d), lambda i: (i, 0)),
            pl.BlockSpec((1, d), lambda i: (0, 0)),
            pl.BlockSpec((d, n_in), lambda i: (0, 0)),
        ],
        out_specs=[
            pl.BlockSpec((PROJ_ROWS, n_a), lambda i: (i, 0)),
            pl.BlockSpec((PROJ_ROWS, n_in - n_a), lambda i: (i, 0)),
        ],
        out_shape=[jax.ShapeDtypeStruct((n, n_a), F32),
                   jax.ShapeDtypeStruct((n, n_in - n_a), F32)],
        compiler_params=_cparams(1),
        name="proj_in",
    )(h, g, w)


def _proj_out_kernel(h_ref, ya_ref, yrw_ref, w_ref, g_ref, o_ref):
    na = ya_ref.shape[1]
    m = jnp.dot(ya_ref[...], w_ref[:na, :], preferred_element_type=F32)
    m = m + jnp.dot(yrw_ref[...], w_ref[na:, :], preferred_element_type=F32)
    o_ref[...] = h_ref[...] + _rms(m, g_ref[...])


def _proj_out(h, y_a, y_rw, w, g):
    n, d = h.shape
    na, nrw = y_a.shape[1], y_rw.shape[1]
    return pl.pallas_call(
        _proj_out_kernel,
        grid=(n // PROJ_ROWS,),
        in_specs=[
            pl.BlockSpec((PROJ_ROWS, d), lambda i: (i, 0)),
            pl.BlockSpec((PROJ_ROWS, na), lambda i: (i, 0)),
            pl.BlockSpec((PROJ_ROWS, nrw), lambda i: (i, 0)),
            pl.BlockSpec((na + nrw, d), lambda i: (0, 0)),
            pl.BlockSpec((1, d), lambda i: (0, 0)),
        ],
        out_specs=pl.BlockSpec((PROJ_ROWS, d), lambda i: (i, 0)),
        out_shape=jax.ShapeDtypeStruct((n, d), F32),
        compiler_params=_cparams(1),
        name="proj_out",
    )(h, y_a, y_rw, w, g)


def _group_rms(x, g, ones_bd):
    ms = _mm(x * x, ones_bd) * (1.0 / HEAD)
    return x * lax.rsqrt(ms + RMS_EPS) * g


def _lrusc_kernel(p_ref, cw_ref, cb_ref, wa_ref, ba_ref, wx_ref, bx_ref, lam_ref, lg_ref,
                  scw_ref, scg_ref, bd_ref, o_ref, xh_ref, ch_ref, hs_ref):
    w = cb_ref.shape[1]
    rows = p_ref.shape[0]

    @pl.when(pl.program_id(1) == 0)
    def _():
        xh_ref[...] = jnp.zeros_like(xh_ref)
        ch_ref[...] = jnp.zeros_like(ch_ref)
        hs_ref[...] = jnp.zeros_like(hs_ref)

    xb = p_ref[:, 0:w]
    gb = p_ref[:, w:2 * w]
    sc_b = p_ref[:, 2 * w:3 * w]
    cx = p_ref[:, 3 * w:4 * w] * p_ref[:, 4 * w:5 * w]
    ones_bd = bd_ref[...]

    xx = jnp.concatenate([xh_ref[...], xb], axis=0)
    u = xx[HALO - 3:HALO - 3 + rows] * cw_ref[0:1, :]
    for k in range(1, 4):
        u = u + xx[HALO - 3 + k:HALO - 3 + k + rows] * cw_ref[k:k + 1, :]
    u = u + cb_ref[...]
    xh_ref[...] = xb[rows - HALO:, :]

    r = jax.nn.sigmoid(_mm(u, wa_ref[...]) + ba_ref[...])
    gi = jax.nn.sigmoid(_mm(u, wx_ref[...]) + bx_ref[...])
    log_a = -LRU_C * r * _softplus(-lam_ref[...])
    a = jnp.exp(log_a)
    th = jnp.tanh(log_a)
    b = jnp.sqrt(-2.0 * th / (1.0 - th)) * (gi * u)

    s = 1
    while s < rows:
        b = a * _row_shift(b, s, 0.0) + b
        a = a * _row_shift(a, s, 1.0)
        s *= 2
    hseq = b + a * hs_ref[...]
    hs_ref[...] = hseq[rows - 1:rows, :]
    y_lru = _group_rms(jax.nn.gelu(gb) * hseq, lg_ref[...], ones_bd)

    cc = jnp.concatenate([ch_ref[...], cx], axis=0)
    conv = cc[HALO - 2:HALO - 2 + rows] * scw_ref[0:1, :]
    for k in range(1, 3):
        conv = conv + cc[HALO - 2 + k:HALO - 2 + k + rows] * scw_ref[k:k + 1, :]
    ch_ref[...] = cx[rows - HALO:, :]
    y_sc = _group_rms(sc_b * conv, scg_ref[...], ones_bd)

    o_ref[:, 0:w] = y_lru.astype(o_ref.dtype)
    o_ref[:, w:2 * w] = y_sc.astype(o_ref.dtype)


def _lrusc(p_a, params, batch, ones_bd):
    n, na = p_a.shape
    w = na // 5
    nt = n // batch // LRU_ROWS
    row = lambda b, t: (b * nt + t, 0)
    fixed = lambda b, t: (0, 0)
    in_specs = [pl.BlockSpec((LRU_ROWS, na), row)]
    in_specs += [pl.BlockSpec(p.shape, fixed) for p in params]
    in_specs += [pl.BlockSpec(ones_bd.shape, fixed)]
    return pl.pallas_call(
        _lrusc_kernel,
        grid=(batch, nt),
        in_specs=in_specs,
        out_specs=pl.BlockSpec((LRU_ROWS, 2 * w), row),
        out_shape=jax.ShapeDtypeStruct((n, 2 * w), BF16),
        scratch_shapes=[pltpu.VMEM((HALO, w), F32), pltpu.VMEM((HALO, w), F32),
                        pltpu.VMEM((1, w), F32)],
        compiler_params=_cparams(2),
        name="lru_shortconv",
    )(p_a, *params, ones_bd)


def _solve_unit_lower(lmats, rhss, same_blk, eye):
    lds = [jnp.where(same_blk, l, 0.0) for l in lmats]
    los = [l - ld for l, ld in zip(lmats, lds)]
    ts = [eye + ld for ld in lds]
    pws = lds
    s = 2
    while s < INV_BLOCK:
        pws = [_mm(pw, pw) for pw in pws]
        ts = [t + _mm(t, pw) for t, pw in zip(ts, pws)]
        s *= 2
    xs = [_mm(t, r) for t, r in zip(ts, rhss)]
    pws = [_mm(t, lo) for t, lo in zip(ts, los)]
    nb = CHUNK // INV_BLOCK
    s = 1
    while s < nb:
        xs = [x + _mm(pw, x) for x, pw in zip(xs, pws)]
        s *= 2
        if s < nb:
            pws = [_mm(pw, pw) for pw in pws]
    return xs


def _rwkv_kernel(p_ref, mu_ref, w0_ref, a0_ref, wl_ref, kk_ref, ka_ref, rk_ref, lw_ref, lb_ref,
                 bd_ref, o_ref, zl_ref, s_ref, *, n_dl, n_il):
    width = w0_ref.shape[1]
    n_heads = width // HEAD
    n_slots = p_ref.shape[0]
    n_chunks = p_ref.shape[1] // CHUNK

    @pl.when(pl.program_id(1) == 0)
    def _():
        zl_ref[...] = jnp.zeros_like(zl_ref)
        s_ref[...] = jnp.zeros_like(s_ref)

    c2 = 2 * CHUNK
    ri = lax.broadcasted_iota(jnp.int32, (c2, c2), 0)
    ci = lax.broadcasted_iota(jnp.int32, (c2, c2), 1)
    cm = jnp.bitwise_and(ci, CHUNK - 1)
    gmask = ((ri < CHUNK) & (cm < ri)) | ((ri >= CHUNK) & (cm <= ri - CHUNK))
    ti = lax.broadcasted_iota(jnp.int32, (CHUNK, CHUNK), 0)
    si = lax.broadcasted_iota(jnp.int32, (CHUNK, CHUNK), 1)
    shift = INV_BLOCK.bit_length() - 1
    same_blk = jnp.right_shift(ti, shift) == jnp.right_shift(si, shift)
    eye = (ti == si).astype(F32)
    tri = (si <= ti).astype(BF16)
    lane = lax.broadcasted_iota(jnp.int32, (CHUNK, wl_ref.shape[0]), 1)
    ones_bd = bd_ref[...]

    def seg_sum(x):
        return _mm(x, ones_bd)

    heads = range(n_heads)
    hsl = [slice(h * HEAD, (h + 1) * HEAD) for h in heads]

    def prepare(slot, r0):
        z = p_ref[slot, pl.ds(r0, CHUNK), :]
        zprev = jnp.where(lax.broadcasted_iota(jnp.int32, z.shape, 0) >= 1,
                          pltpu.roll(z, 1, axis=0), zl_ref[slot])
        zl_ref[slot] = z[CHUNK - 1:CHUNK, :]
        zz = z + (zprev - z) * mu_ref[...]
        r = zz[:, 0:width]
        k = zz[:, width:2 * width]
        v = zz[:, 2 * width:3 * width]
        lo = zz[:, 3 * width:]
        lo_act = jnp.where(lane < n_dl, jnp.tanh(lo),
                           jnp.where(lane < n_dl + n_il, lo, jax.nn.sigmoid(lo)))
        lora = _mm(lo_act, wl_ref[...])
        w_log = -_softplus(-(w0_ref[...] + lora[:, 0:width])) - 0.5
        lw = -jnp.exp(w_log)
        a = jax.nn.sigmoid(a0_ref[...] + lora[:, width:2 * width])
        g = lora[:, 2 * width:3 * width]
        kk = k * kk_ref[...]
        kf = k * (1.0 + (a - 1.0) * ka_ref[...])
        kk = kk * lax.rsqrt(jnp.maximum(seg_sum(kk * kk), 1e-24))
        bonus = seg_sum(r * kf * rk_ref[...])
        kka = kk * a

        lw_hi = lw.astype(BF16)
        lw_lo = (lw - lw_hi.astype(F32)).astype(BF16)
        cum = (jnp.dot(tri, lw_hi, preferred_element_type=F32)
               + jnp.dot(tri, lw_lo, preferred_element_type=F32))
        cl = cum[CHUNK - 1:CHUNK, :]
        w_in = jnp.exp(-cum)
        w_rem = jnp.exp(cl - cum)
        al = (-kk * jnp.exp(cum - lw)).astype(BF16)
        be = (kka * w_in).astype(BF16)
        kt = (kf * w_in).astype(BF16)
        rt = (r * jnp.exp(cum)).astype(BF16)
        bw = (kka * w_rem).astype(BF16)
        kw = (kf * w_rem).astype(BF16)
        vb = v.astype(BF16)
        w_c = jnp.exp(cl)
        ops = dict(
            ar=[jnp.concatenate([al[:, hs], rt[:, hs]], axis=0) for hs in hsl],
            bk=[jnp.concatenate([be[:, hs], kt[:, hs]], axis=0) for hs in hsl],
            bkw=[jnp.concatenate([bw[:, hs], kw[:, hs]], axis=0) for hs in hsl],
            v=[vb[:, hs] for hs in hsl],
            wc=[w_c[:, hs] for hs in hsl])
        return ops, (v, g, bonus)

    def finish(slot, r0, ys, tail):
        v, g, bonus = tail
        y = jnp.concatenate(ys, axis=1)
        mean = seg_sum(y) * (1.0 / HEAD)
        d = y - mean
        var = seg_sum(d * d) * (1.0 / HEAD)
        yn = d * lax.rsqrt(var + LNX_EPS) * lw_ref[...] + lb_ref[...]
        o_ref[slot, pl.ds(r0, CHUNK), :] = ((yn + bonus * v) * g).astype(o_ref.dtype)

    def chunk_body(c, carry):
        r0 = pl.multiple_of(c * CHUNK, CHUNK)
        prepared = [prepare(slot, r0) for slot in range(n_slots)]
        flat = lambda key: [x for ops, _ in prepared for x in ops[key]]
        ars, bks, bkws, vhs, wcs = (flat(k) for k in ("ar", "bk", "bkw", "v", "wc"))
        n_chains = n_slots * n_heads
        s0s = [s_ref[i] for i in range(n_chains)]
        gms = [jnp.where(gmask, _mm_nt(ar, bk), 0.0) for ar, bk in zip(ars, bks)]
        a_ss = [_mm_nt(ar, s0) for ar, s0 in zip(ars, s0s)]
        akvs = [_mm(gm[0:CHUNK, CHUNK:], v_h) for gm, v_h in zip(gms, vhs)]
        rhss = [a_s[0:CHUNK] + akv for a_s, akv in zip(a_ss, akvs)]
        es = _solve_unit_lower([gm[0:CHUNK, 0:CHUNK] for gm in gms], rhss, same_blk, eye)
        evs = [jnp.concatenate([e.astype(BF16), v_h], axis=0) for e, v_h in zip(es, vhs)]
        ys = [a_s[CHUNK:] + _mm(gm[CHUNK:, :], ev) for a_s, gm, ev in zip(a_ss, gms, evs)]
        for i in range(n_chains):
            s_ref[i] = s0s[i] * wcs[i] + _mm_tn(evs[i], bkws[i])
        for slot in range(n_slots):
            finish(slot, r0, ys[slot * n_heads:(slot + 1) * n_heads], prepared[slot][1])
        return carry

    lax.fori_loop(0, n_chunks, chunk_body, 0)


def _rwkv(p_rw, params, batch, ones_bd, n_dl, n_il):
    n, nrw = p_rw.shape
    width = params[1].shape[1]
    t_pad = n // batch
    blk = lambda b, t: (b, t, 0)
    fixed = lambda b, t: (0, 0)
    in_specs = [pl.BlockSpec((RWKV_SLOTS, RWKV_ROWS, nrw), blk)]
    in_specs += [pl.BlockSpec(p.shape, fixed) for p in params]
    in_specs += [pl.BlockSpec(ones_bd.shape, fixed)]
    y = pl.pallas_call(
        functools.partial(_rwkv_kernel, n_dl=n_dl, n_il=n_il),
        grid=(batch // RWKV_SLOTS, t_pad // RWKV_ROWS),
        in_specs=in_specs,
        out_specs=pl.BlockSpec((RWKV_SLOTS, RWKV_ROWS, width), blk),
        out_shape=jax.ShapeDtypeStruct((batch, t_pad, width), BF16),
        scratch_shapes=[pltpu.VMEM((RWKV_SLOTS, 1, nrw), F32),
                        pltpu.VMEM((RWKV_SLOTS * (width // HEAD), HEAD, HEAD), F32)],
        compiler_params=_cparams(2),
        name="rwkv7",
    )(p_rw.reshape(batch, t_pad, nrw), *params, ones_bd)
    return y.reshape(n, width)


def _ones_block_diag(width):
    return jnp.asarray(np.kron(np.eye(width // HEAD), np.ones((HEAD, HEAD))), BF16)


def _block_diag(w):
    g, i, j = w.shape
    return jnp.einsum('gij,gh->gihj', w, jnp.eye(g, dtype=w.dtype)).reshape(g * i, g * j)


def kernel(x, meta_tokens, norm_g, ffn1_w_in, ffn1_w_out, ffn2_w_in, ffn2_w_out, mix_w_in, mix_w_out, lru_conv_w, lru_conv_b, lru_wa, lru_ba, lru_wx, lru_bx, lru_lambda, lru_norm_g, sc_conv_w, sc_norm_g, rwkv_mu, rwkv_w0, rwkv_w2, rwkv_a0, rwkv_a2, rwkv_g2, rwkv_k_k, rwkv_k_a, rwkv_r_k, rwkv_lnx_w, rwkv_lnx_b):
    bsz, seq, d = x.shape
    n_meta = meta_tokens.shape[0]
    depth = norm_g.shape[0]
    lru_w = lru_conv_b.shape[1]
    sc_w = sc_norm_g.shape[1]
    rw_w = rwkv_w0.shape[1]
    assert lru_w == sc_w
    n_a = 2 * lru_w + 3 * sc_w
    t_real = n_meta + seq
    t_pad = -(-t_real // TIME_ALIGN) * TIME_ALIGN

    meta = jnp.broadcast_to(meta_tokens.astype(x.dtype)[None], (bsz, n_meta, d))
    pad = jnp.zeros((bsz, t_pad - t_real, d), x.dtype)
    h = jnp.concatenate([meta, x, pad], axis=1).reshape(bsz * t_pad, d)

    bd_a = _ones_block_diag(lru_w)
    bd_rw = _ones_block_diag(rw_w)
    row = lambda p: p.reshape(1, -1)

    for l in range(depth):
        g = norm_g[l]
        h = _ffn(h, row(g[0]), ffn1_w_in[l].astype(BF16), ffn1_w_out[l].astype(BF16), row(g[1]))

        p_a, p_rw = _proj_in(h, row(g[2]), mix_w_in[l].astype(BF16), n_a)
        lru_params = (lru_conv_w[l], row(lru_conv_b[l]),
                      _block_diag(lru_wa[l]).astype(BF16), row(lru_ba[l]),
                      _block_diag(lru_wx[l]).astype(BF16), row(lru_bx[l]),
                      row(lru_lambda[l]), row(lru_norm_g[l]), sc_conv_w[l], row(sc_norm_g[l]))
        y_a = _lrusc(p_a, lru_params, bsz, bd_a)

        n_dl, n_il, n_gl = rwkv_w2.shape[1], rwkv_a2.shape[1], rwkv_g2.shape[1]
        lora_w = jnp.zeros((n_dl + n_il + n_gl, 3 * rw_w), F32)
        lora_w = lora_w.at[0:n_dl, 0:rw_w].set(rwkv_w2[l])
        lora_w = lora_w.at[n_dl:n_dl + n_il, rw_w:2 * rw_w].set(rwkv_a2[l])
        lora_w = lora_w.at[n_dl + n_il:, 2 * rw_w:].set(rwkv_g2[l])
        rw_params = (row(rwkv_mu[l]), row(rwkv_w0[l]), row(rwkv_a0[l]), lora_w.astype(BF16),
                     row(rwkv_k_k[l]), row(rwkv_k_a[l]), row(rwkv_r_k[l]),
                     row(rwkv_lnx_w[l]), row(rwkv_lnx_b[l]))
        y_rw = _rwkv(p_rw, rw_params, bsz, bd_rw, n_dl, n_il)

        h = _proj_out(h, y_a, y_rw, mix_w_out[l].astype(BF16), row(g[3]))
        h = _ffn(h, row(g[4]), ffn2_w_in[l].astype(BF16), ffn2_w_out[l].astype(BF16), row(g[5]))

    return h.reshape(bsz, t_pad, d)[:, n_meta:t_real]
```

```python
import functools

import jax
import jax.numpy as jnp
import numpy as np
from jax import lax
from jax.experimental import pallas as pl
from jax.experimental.pallas import tpu as pltpu

F32 = jnp.float32
BF16 = jnp.bfloat16

RMS_EPS = 1e-6
LNX_EPS = 64e-5
LRU_C = 8.0

HEAD = 64
CHUNK = 64
INV_BLOCK = 16
RWKV_ROWS = 192
RWKV_SLOTS = 4
LRU_ROWS = 192
FFN_ROWS = 528
TIME_ALIGN = 2112
SUBLANES = 8
HALO = 8
VMEM_LIMIT = 56 * 1024 * 1024


def _cparams(n_axes):
    return pltpu.CompilerParams(
        dimension_semantics=("arbitrary",) * n_axes, vmem_limit_bytes=VMEM_LIMIT)


def _rms(x, g):
    return x * lax.rsqrt(jnp.mean(x * x, axis=-1, keepdims=True) + RMS_EPS) * g


def _mm(a, b):
    return jnp.dot(a.astype(BF16), b.astype(BF16), preferred_element_type=F32)


def _mm_nt(a, b):
    return lax.dot_general(a.astype(BF16), b.astype(BF16), (((1,), (1,)), ((), ())),
                           preferred_element_type=F32)


def _mm_tn(a, b):
    return lax.dot_general(a.astype(BF16), b.astype(BF16), (((0,), (0,)), ((), ())),
                           preferred_element_type=F32)


def _softplus(x):
    return jnp.maximum(x, 0.0) + jnp.log1p(jnp.exp(-jnp.abs(x)))


def _resident(shape):
    return pl.BlockSpec(shape, lambda *_: (0,) * len(shape), pipeline_mode=pl.Buffered(1))


def _ffn_kernel(*refs, has_mix):
    if has_mix:
        h_ref, ya_ref, yrw_ref, wm_ref, gm_ref, gpre_ref, wi_ref, wo_ref, gpost_ref, o_ref = refs
        na = ya_ref.shape[1]
        m = jnp.dot(ya_ref[...], wm_ref[:na, :], preferred_element_type=F32)
        m = m + jnp.dot(yrw_ref[...], wm_ref[na:, :], preferred_element_type=F32)
        h = h_ref[...] + _rms(m, gm_ref[...])
    else:
        h_ref, gpre_ref, wi_ref, wo_ref, gpost_ref, o_ref = refs
        h = h_ref[...]
    d_ff = wo_ref.shape[0]
    xn = _rms(h, gpre_ref[...]).astype(BF16)
    gu = jnp.dot(xn, wi_ref[...], preferred_element_type=F32)
    act = (jax.nn.silu(gu[:, :d_ff]) * gu[:, d_ff:]).astype(BF16)
    y = jnp.dot(act, wo_ref[...], preferred_element_type=F32)
    o_ref[...] = h + 0.5 * _rms(y, gpost_ref[...])


def _ffn(h, g_pre, w_in, w_out, g_post, mix=None):
    n, d = h.shape
    d_ff = w_out.shape[0]
    rows = lambda width: pl.BlockSpec((FFN_ROWS, width), lambda i: (i, 0))
    operands, in_specs = [h], [rows(d)]
    if mix is not None:
        y_a, y_rw, w_mix, g_mix = mix
        operands += [y_a, y_rw, w_mix, g_mix]
        in_specs += [rows(y_a.shape[1]), rows(y_rw.shape[1]),
                     _resident(w_mix.shape), _resident(g_mix.shape)]
    operands += [g_pre, w_in, w_out, g_post]
    in_specs += [_resident((1, d)), _resident((d, 2 * d_ff)), _resident((d_ff, d)),
                 _resident((1, d))]
    return pl.pallas_call(
        functools.partial(_ffn_kernel, has_mix=mix is not None),
        grid=(n // FFN_ROWS,),
        in_specs=in_specs,
        out_specs=rows(d),
        out_shape=jax.ShapeDtypeStruct((n, d), F32),
        compiler_params=_cparams(1),
        name="ffn_mix" if mix is not None else "ffn",
    )(*operands)


def _group_rms(x, g, ones_bd):
    ms = _mm(x * x, ones_bd) * (1.0 / HEAD)
    return x * lax.rsqrt(ms + RMS_EPS) * g


def _lrusc_kernel(h_ref, gin_ref, win_ref, cw_ref, cb_ref, wa_ref, ba_ref, wx_ref, bx_ref,
                  lam_ref, lg_ref, scw_ref, scg_ref, bd_ref, o_ref, xh_ref, ch_ref, hs_ref):
    w = cb_ref.shape[1]
    rows = h_ref.shape[0]

    @pl.when(pl.program_id(1) == 0)
    def _():
        xh_ref[...] = jnp.zeros_like(xh_ref)
        ch_ref[...] = jnp.zeros_like(ch_ref)
        hs_ref[...] = jnp.zeros_like(hs_ref)

    p = jnp.dot(_rms(h_ref[...], gin_ref[...]).astype(BF16), win_ref[...],
                preferred_element_type=F32)
    xb = p[:, 0:w]
    gb = p[:, w:2 * w]
    sc_b = p[:, 2 * w:3 * w]
    cx = p[:, 3 * w:4 * w] * p[:, 4 * w:5 * w]
    ones_bd = bd_ref[...]

    xx = jnp.concatenate([xh_ref[...], xb], axis=0)
    u = xx[HALO - 3:HALO - 3 + rows] * cw_ref[0:1, :]
    for k in range(1, 4):
        u = u + xx[HALO - 3 + k:HALO - 3 + k + rows] * cw_ref[k:k + 1, :]
    u = u + cb_ref[...]
    xh_ref[...] = xb[rows - HALO:, :]

    r = jax.nn.sigmoid(_mm(u, wa_ref[...]) + ba_ref[...])
    gi = jax.nn.sigmoid(_mm(u, wx_ref[...]) + bx_ref[...])
    log_a = -LRU_C * r * _softplus(-lam_ref[...])
    a = jnp.exp(log_a)
    th = jnp.tanh(log_a)
    b = jnp.sqrt(-2.0 * th / (1.0 - th)) * (gi * u)

    sub = jnp.bitwise_and(lax.broadcasted_iota(jnp.int32, a.shape, 0), SUBLANES - 1)
    s = 1
    while s < SUBLANES:
        b = a * jnp.where(sub >= s, pltpu.roll(b, s, axis=0), 0.0) + b
        a = a * jnp.where(sub >= s, pltpu.roll(a, s, axis=0), 1.0)
        s *= 2
    carry = hs_ref[...]
    groups = []
    for r0 in range(0, rows, SUBLANES):
        hg = b[r0:r0 + SUBLANES] + a[r0:r0 + SUBLANES] * carry
        groups.append(hg)
        carry = hg[SUBLANES - 1:SUBLANES, :]
    hseq = jnp.concatenate(groups, axis=0)
    hs_ref[...] = carry
    y_lru = _group_rms(jax.nn.gelu(gb) * hseq, lg_ref[...], ones_bd)

    cc = jnp.concatenate([ch_ref[...], cx], axis=0)
    conv = cc[HALO - 2:HALO - 2 + rows] * scw_ref[0:1, :]
    for k in range(1, 3):
        conv = conv + cc[HALO - 2 + k:HALO - 2 + k + rows] * scw_ref[k:k + 1, :]
    ch_ref[...] = cx[rows - HALO:, :]
    y_sc = _group_rms(sc_b * conv, scg_ref[...], ones_bd)

    o_ref[:, 0:w] = y_lru.astype(o_ref.dtype)
    o_ref[:, w:2 * w] = y_sc.astype(o_ref.dtype)


def _lrusc(h, params, batch, ones_bd):
    n, d = h.shape
    w = params[3].shape[1]
    nt = n // batch // LRU_ROWS
    row = lambda b, t: (b * nt + t, 0)
    in_specs = [pl.BlockSpec((LRU_ROWS, d), row)]
    in_specs += [_resident(p.shape) for p in params]
    in_specs += [_resident(ones_bd.shape)]
    return pl.pallas_call(
        _lrusc_kernel,
        grid=(batch, nt),
        in_specs=in_specs,
        out_specs=pl.BlockSpec((LRU_ROWS, 2 * w), row),
        out_shape=jax.ShapeDtypeStruct((n, 2 * w), BF16),
        scratch_shapes=[pltpu.VMEM((HALO, w), F32), pltpu.VMEM((HALO, w), F32),
                        pltpu.VMEM((1, w), F32)],
        compiler_params=_cparams(2),
        name="lru_shortconv",
    )(h, *params, ones_bd)


def _solve_unit_lower(lmats, rhss, same_blk, eye):
    lds = [jnp.where(same_blk, l, 0.0) for l in lmats]
    los = [l - ld for l, ld in zip(lmats, lds)]
    ts = [eye + ld for ld in lds]
    pws = lds
    s = 2
    while s < INV_BLOCK:
        pws = [_mm(pw, pw) for pw in pws]
        ts = [t + _mm(t, pw) for t, pw in zip(ts, pws)]
        s *= 2
    xs = [_mm(t, r) for t, r in zip(ts, rhss)]
    pws = [_mm(t, lo) for t, lo in zip(ts, los)]
    nb = CHUNK // INV_BLOCK
    s = 1
    while s < nb:
        xs = [x + _mm(pw, x) for x, pw in zip(xs, pws)]
        s *= 2
        if s < nb:
            pws = [_mm(pw, pw) for pw in pws]
    return xs


def _rwkv_kernel(h_ref, gin_ref, win_ref, mu_ref, w0_ref, a0_ref, wl_ref, kk_ref, ka_ref, rk_ref,
                 lw_ref, lb_ref, bd_ref, o_ref, zl_ref, s_ref, p_ref, *, n_dl, n_il):
    width = w0_ref.shape[1]
    n_heads = width // HEAD
    n_slots = h_ref.shape[0]
    n_chunks = h_ref.shape[1] // CHUNK

    @pl.when(pl.program_id(1) == 0)
    def _():
        zl_ref[...] = jnp.zeros_like(zl_ref)
        s_ref[...] = jnp.zeros_like(s_ref)

    for slot in range(n_slots):
        p_ref[slot] = jnp.dot(_rms(h_ref[slot], gin_ref[...]).astype(BF16), win_ref[...],
                              preferred_element_type=F32)

    c2 = 2 * CHUNK
    ri = lax.broadcasted_iota(jnp.int32, (c2, c2), 0)
    ci = lax.broadcasted_iota(jnp.int32, (c2, c2), 1)
    cm = jnp.bitwise_and(ci, CHUNK - 1)
    gmask = ((ri < CHUNK) & (cm < ri)) | ((ri >= CHUNK) & (cm <= ri - CHUNK))
    ti = lax.broadcasted_iota(jnp.int32, (CHUNK, CHUNK), 0)
    si = lax.broadcasted_iota(jnp.int32, (CHUNK, CHUNK), 1)
    shift = INV_BLOCK.bit_length() - 1
    same_blk = jnp.right_shift(ti, shift) == jnp.right_shift(si, shift)
    eye = (ti == si).astype(F32)
    tri = (si <= ti).astype(BF16)
    lane = lax.broadcasted_iota(jnp.int32, (CHUNK, wl_ref.shape[0]), 1)
    ones_bd = bd_ref[...]

    bd_w = ones_bd.shape[0]

    def seg_sum(x):
        xb = x.astype(BF16)
        parts = [jnp.dot(xb[:, c:c + bd_w], ones_bd, preferred_element_type=F32)
                 for c in range(0, x.shape[1], bd_w)]
        return jnp.concatenate(parts, axis=1)

    heads = range(n_heads)
    hsl = [slice(h * HEAD, (h + 1) * HEAD) for h in heads]

    def prepare(slot, r0):
        z = p_ref[slot, pl.ds(r0, CHUNK), :]
        zprev = jnp.where(lax.broadcasted_iota(jnp.int32, z.shape, 0) >= 1,
                          pltpu.roll(z, 1, axis=0), zl_ref[slot])
        zl_ref[slot] = z[CHUNK - 1:CHUNK, :]
        zz = z + (zprev - z) * mu_ref[...]
        r = zz[:, 0:width]
        k = zz[:, width:2 * width]
        v = zz[:, 2 * width:3 * width]
        lo = zz[:, 3 * width:]
        lo_act = jnp.where(lane < n_dl, jnp.tanh(lo),
                           jnp.where(lane < n_dl + n_il, lo, jax.nn.sigmoid(lo)))
        lora = _mm(lo_act, wl_ref[...])
        w_log = -_softplus(-(w0_ref[...] + lora[:, 0:width])) - 0.5
        lw = -jnp.exp(w_log)
        a = jax.nn.sigmoid(a0_ref[...] + lora[:, width:2 * width])
        g = lora[:, 2 * width:3 * width]
        kk = k * kk_ref[...]
        kf = k * (1.0 + (a - 1.0) * ka_ref[...])
        kk = kk * lax.rsqrt(jnp.maximum(seg_sum(kk * kk), 1e-24))
        bonus = seg_sum(r * kf * rk_ref[...])
        kka = kk * a

        lw_hi = lw.astype(BF16)
        lw_lo = (lw - lw_hi.astype(F32)).astype(BF16)
        cum = (jnp.dot(tri, lw_hi, preferred_element_type=F32)
               + jnp.dot(tri, lw_lo, preferred_element_type=F32))
        cl = cum[CHUNK - 1:CHUNK, :]
        w_in = jnp.exp(-cum)
        w_rem = jnp.exp(cl - cum)
        al = (-kk * jnp.exp(cum - lw)).astype(BF16)
        be = (kka * w_in).astype(BF16)
        kt = (kf * w_in).astype(BF16)
        rt = (r * jnp.exp(cum)).astype(BF16)
        bw = (kka * w_rem).astype(BF16)
        kw = (kf * w_rem).astype(BF16)
        vb = v.astype(BF16)
        w_c = jnp.exp(cl)
        ops = dict(
            ar=[jnp.concatenate([al[:, hs], rt[:, hs]], axis=0) for hs in hsl],
            bk=[jnp.concatenate([be[:, hs], kt[:, hs]], axis=0) for hs in hsl],
            bkw=[jnp.concatenate([bw[:, hs], kw[:, hs]], axis=0) for hs in hsl],
            v=[vb[:, hs] for hs in hsl],
            wc=[w_c[:, hs] for hs in hsl])
        return ops, (v, g, bonus)

    def finish(slot, r0, ys, tail):
        v, g, bonus = tail
        y = jnp.concatenate(ys, axis=1)
        mean = seg_sum(y) * (1.0 / HEAD)
        d = y - mean
        var = seg_sum(d * d) * (1.0 / HEAD)
        yn = d * lax.rsqrt(var + LNX_EPS) * lw_ref[...] + lb_ref[...]
        o_ref[slot, pl.ds(r0, CHUNK), :] = ((yn + bonus * v) * g).astype(o_ref.dtype)

    def chunk_body(c, carry):
        r0 = pl.multiple_of(c * CHUNK, CHUNK)
        prepared = [prepare(slot, r0) for slot in range(n_slots)]
        flat = lambda key: [x for ops, _ in prepared for x in ops[key]]
        ars, bks, bkws, vhs, wcs = (flat(k) for k in ("ar", "bk", "bkw", "v", "wc"))
        n_chains = n_slots * n_heads
        s0s = [s_ref[i] for i in range(n_chains)]
        gms = [jnp.where(gmask, _mm_nt(ar, bk), 0.0) for ar, bk in zip(ars, bks)]
        a_ss = [_mm_nt(ar, s0) for ar, s0 in zip(ars, s0s)]
        akvs = [_mm(gm[0:CHUNK, CHUNK:], v_h) for gm, v_h in zip(gms, vhs)]
        rhss = [a_s[0:CHUNK] + akv for a_s, akv in zip(a_ss, akvs)]
        es = _solve_unit_lower([gm[0:CHUNK, 0:CHUNK] for gm in gms], rhss, same_blk, eye)
        evs = [jnp.concatenate([e.astype(BF16), v_h], axis=0) for e, v_h in zip(es, vhs)]
        ys = [a_s[CHUNK:] + _mm(gm[CHUNK:, :], ev) for a_s, gm, ev in zip(a_ss, gms, evs)]
        for i in range(n_chains):
            s_ref[i] = s0s[i] * wcs[i] + _mm_tn(evs[i], bkws[i])
        for slot in range(n_slots):
            finish(slot, r0, ys[slot * n_heads:(slot + 1) * n_heads], prepared[slot][1])
        return carry

    lax.fori_loop(0, n_chunks, chunk_body, 0)


def _rwkv(h, params, batch, ones_bd, n_dl, n_il):
    n, d = h.shape
    nrw = params[1].shape[1]
    width = params[3].shape[1]
    t_pad = n // batch
    blk = lambda b, t: (b, t, 0)
    in_specs = [pl.BlockSpec((RWKV_SLOTS, RWKV_ROWS, d), blk)]
    in_specs += [_resident(p.shape) for p in params]
    in_specs += [_resident(ones_bd.shape)]
    y = pl.pallas_call(
        functools.partial(_rwkv_kernel, n_dl=n_dl, n_il=n_il),
        grid=(batch // RWKV_SLOTS, t_pad // RWKV_ROWS),
        in_specs=in_specs,
        out_specs=pl.BlockSpec((RWKV_SLOTS, RWKV_ROWS, width), blk),
        out_shape=jax.ShapeDtypeStruct((batch, t_pad, width), BF16),
        scratch_shapes=[pltpu.VMEM((RWKV_SLOTS, 1, nrw), F32),
                        pltpu.VMEM((RWKV_SLOTS * (width // HEAD), HEAD, HEAD), F32),
                        pltpu.VMEM((RWKV_SLOTS, RWKV_ROWS, nrw), F32)],
        compiler_params=_cparams(2),
        name="rwkv7",
    )(h.reshape(batch, t_pad, d), *params, ones_bd)
    return y.reshape(n, width)


def _ones_block_diag(width):
    return jnp.asarray(np.kron(np.eye(width // HEAD), np.ones((HEAD, HEAD))), BF16)


def _block_diag(w):
    g, i, j = w.shape
    return jnp.einsum('gij,gh->gihj', w, jnp.eye(g, dtype=w.dtype)).reshape(g * i, g * j)


def kernel(x, meta_tokens, norm_g, ffn1_w_in, ffn1_w_out, ffn2_w_in, ffn2_w_out, mix_w_in, mix_w_out, lru_conv_w, lru_conv_b, lru_wa, lru_ba, lru_wx, lru_bx, lru_lambda, lru_norm_g, sc_conv_w, sc_norm_g, rwkv_mu, rwkv_w0, rwkv_w2, rwkv_a0, rwkv_a2, rwkv_g2, rwkv_k_k, rwkv_k_a, rwkv_r_k, rwkv_lnx_w, rwkv_lnx_b):
    bsz, seq, d = x.shape
    n_meta = meta_tokens.shape[0]
    depth = norm_g.shape[0]
    lru_w = lru_conv_b.shape[1]
    sc_w = sc_norm_g.shape[1]
    rw_w = rwkv_w0.shape[1]
    assert lru_w == sc_w
    n_a = 2 * lru_w + 3 * sc_w
    t_real = n_meta + seq
    t_pad = -(-t_real // TIME_ALIGN) * TIME_ALIGN

    meta = jnp.broadcast_to(meta_tokens.astype(x.dtype)[None], (bsz, n_meta, d))
    pad = jnp.zeros((bsz, t_pad - t_real, d), x.dtype)
    h = jnp.concatenate([meta, x, pad], axis=1).reshape(bsz * t_pad, d)

    ones_bd = _ones_block_diag(lru_w)
    row = lambda p: p.reshape(1, -1)

    for l in range(depth):
        g = norm_g[l]
        h = _ffn(h, row(g[0]), ffn1_w_in[l].astype(BF16), ffn1_w_out[l].astype(BF16), row(g[1]))

        w_in = mix_w_in[l].astype(BF16)
        lru_params = (row(g[2]), w_in[:, :n_a], lru_conv_w[l], row(lru_conv_b[l]),
                      _block_diag(lru_wa[l]).astype(BF16), row(lru_ba[l]),
                      _block_diag(lru_wx[l]).astype(BF16), row(lru_bx[l]),
                      row(lru_lambda[l]), row(lru_norm_g[l]), sc_conv_w[l], row(sc_norm_g[l]))
        y_a = _lrusc(h, lru_params, bsz, ones_bd)

        n_dl, n_il, n_gl = rwkv_w2.shape[1], rwkv_a2.shape[1], rwkv_g2.shape[1]
        lora_w = jnp.zeros((n_dl + n_il + n_gl, 3 * rw_w), F32)
        lora_w = lora_w.at[0:n_dl, 0:rw_w].set(rwkv_w2[l])
        lora_w = lora_w.at[n_dl:n_dl + n_il, rw_w:2 * rw_w].set(rwkv_a2[l])
        lora_w = lora_w.at[n_dl + n_il:, 2 * rw_w:].set(rwkv_g2[l])
        rw_params = (row(g[2]), w_in[:, n_a:], row(rwkv_mu[l]), row(rwkv_w0[l]), row(rwkv_a0[l]),
                     lora_w.astype(BF16), row(rwkv_k_k[l]), row(rwkv_k_a[l]), row(rwkv_r_k[l]),
                     row(rwkv_lnx_w[l]), row(rwkv_lnx_b[l]))
        y_rw = _rwkv(h, rw_params, bsz, ones_bd, n_dl, n_il)

        h = _ffn(h, row(g[4]), ffn2_w_in[l].astype(BF16), ffn2_w_out[l].astype(BF16), row(g[5]),
                 mix=(y_a, y_rw, mix_w_out[l].astype(BF16), row(g[3])))

    return h.reshape(bsz, t_pad, d)[:, n_meta:t_real]
```

```python
import functools

import jax
import jax.numpy as jnp
import numpy as np
from jax import lax
from jax.experimental import pallas as pl
from jax.experimental.pallas import tpu as pltpu

F32 = jnp.float32
BF16 = jnp.bfloat16

RMS_EPS = 1e-6
LNX_EPS = 64e-5
LRU_C = 8.0

HEAD = 64
CHUNK = 64
INV_BLOCK = 16
RWKV_ROWS = 192
RWKV_SLOTS = 4
LRU_ROWS = 192
FFN_ROWS = 528
TIME_ALIGN = 2112
SUBLANES = 8
HALO = 8
VMEM_LIMIT = 56 * 1024 * 1024


def _cparams(n_axes):
    return pltpu.CompilerParams(
        dimension_semantics=("arbitrary",) * n_axes, vmem_limit_bytes=VMEM_LIMIT)


class _Layer:
    def __init__(self, stacked, layer):
        self.stacked, self.layer = stacked, layer
        self.shape = stacked.shape[1:]


def _operand(p):
    return p.stacked if isinstance(p, _Layer) else p


def _resident(p):
    zeros = (0,) * len(p.shape)
    if isinstance(p, _Layer):
        layer = p.layer
        return pl.BlockSpec((None,) + tuple(p.shape), lambda *_: (layer,) + zeros,
                            pipeline_mode=pl.Buffered(1))
    return pl.BlockSpec(p.shape, lambda *_: zeros, pipeline_mode=pl.Buffered(1))


def _rms(x, g):
    return x * lax.rsqrt(jnp.mean(x * x, axis=-1, keepdims=True) + RMS_EPS) * g


def _bf(x):
    return x.astype(BF16)


def _mm(a, b):
    return jnp.dot(_bf(a), _bf(b), preferred_element_type=F32)


def _mm_nt(a, b):
    return lax.dot_general(_bf(a), _bf(b), (((1,), (1,)), ((), ())), preferred_element_type=F32)


def _mm_tn(a, b):
    return lax.dot_general(_bf(a), _bf(b), (((0,), (0,)), ((), ())), preferred_element_type=F32)


def _softplus(x):
    return jnp.maximum(x, 0.0) + jnp.log1p(jnp.exp(-jnp.abs(x)))


def _ffn_kernel(*refs, has_mix):
    if has_mix:
        h_ref, ya_ref, yrw_ref, wm_ref, gm_ref, gpre_ref, wi_ref, wo_ref, gpost_ref, o_ref = refs
        na = ya_ref.shape[1]
        m = jnp.dot(ya_ref[...], wm_ref[:na, :], preferred_element_type=F32)
        m = m + jnp.dot(yrw_ref[...], wm_ref[na:, :], preferred_element_type=F32)
        h = h_ref[...] + _rms(m, gm_ref[...])
    else:
        h_ref, gpre_ref, wi_ref, wo_ref, gpost_ref, o_ref = refs
        h = h_ref[...]
    d_ff = wo_ref.shape[0]
    xn = _bf(_rms(h, gpre_ref[...]))
    gu = jnp.dot(xn, wi_ref[...], preferred_element_type=F32)
    act = _bf(jax.nn.silu(gu[:, :d_ff]) * gu[:, d_ff:])
    y = jnp.dot(act, wo_ref[...], preferred_element_type=F32)
    o_ref[...] = h + 0.5 * _rms(y, gpost_ref[...])


def _ffn(h, g_pre, w_in, w_out, g_post, mix=None):
    n, d = h.shape
    d_ff = w_out.shape[0]
    rows = lambda width: pl.BlockSpec((FFN_ROWS, width), lambda i: (i, 0))
    operands, in_specs = [h], [rows(d)]
    params = (g_pre, w_in, w_out, g_post)
    if mix is not None:
        y_a, y_rw, w_mix, g_mix = mix
        operands += [y_a, y_rw]
        in_specs += [rows(y_a.shape[1]), rows(y_rw.shape[1])]
        params = (w_mix, g_mix) + params
    operands += [_operand(p) for p in params]
    in_specs += [_resident(p) for p in params]
    return pl.pallas_call(
        functools.partial(_ffn_kernel, has_mix=mix is not None),
        grid=(n // FFN_ROWS,),
        in_specs=in_specs,
        out_specs=rows(d),
        out_shape=jax.ShapeDtypeStruct((n, d), F32),
        compiler_params=_cparams(1),
        name="ffn_mix" if mix is not None else "ffn",
    )(*operands)


def _group_rms(x, g, ones_bd):
    ms = _mm(x * x, ones_bd) * (1.0 / HEAD)
    return x * lax.rsqrt(ms + RMS_EPS) * g


def _lrusc_kernel(h_ref, gin_ref, win_ref, cw_ref, cb_ref, wa_ref, ba_ref, wx_ref, bx_ref,
                  lam_ref, lg_ref, scw_ref, scg_ref, bd_ref, o_ref, xh_ref, ch_ref, hs_ref):
    w = cb_ref.shape[1]
    rows = h_ref.shape[0]

    @pl.when(pl.program_id(1) == 0)
    def _():
        xh_ref[...] = jnp.zeros_like(xh_ref)
        ch_ref[...] = jnp.zeros_like(ch_ref)
        hs_ref[...] = jnp.zeros_like(hs_ref)

    p = jnp.dot(_bf(_rms(h_ref[...], gin_ref[...])), win_ref[:, 0:5 * w],
                preferred_element_type=F32)
    xb = p[:, 0:w]
    gb = p[:, w:2 * w]
    sc_b = p[:, 2 * w:3 * w]
    cx = p[:, 3 * w:4 * w] * p[:, 4 * w:5 * w]
    ones_bd = bd_ref[...]

    xx = jnp.concatenate([xh_ref[...], xb], axis=0)
    u = xx[HALO - 3:HALO - 3 + rows] * cw_ref[0:1, :]
    for k in range(1, 4):
        u = u + xx[HALO - 3 + k:HALO - 3 + k + rows] * cw_ref[k:k + 1, :]
    u = u + cb_ref[...]
    xh_ref[...] = xb[rows - HALO:, :]

    r = jax.nn.sigmoid(_mm(u, wa_ref[...]) + ba_ref[...])
    gi = jax.nn.sigmoid(_mm(u, wx_ref[...]) + bx_ref[...])
    log_a = -LRU_C * r * _softplus(-lam_ref[...])
    a = jnp.exp(log_a)
    th = jnp.tanh(log_a)
    b = jnp.sqrt(-2.0 * th / (1.0 - th)) * (gi * u)

    sub = jnp.bitwise_and(lax.broadcasted_iota(jnp.int32, a.shape, 0), SUBLANES - 1)
    s = 1
    while s < SUBLANES:
        b = a * jnp.where(sub >= s, pltpu.roll(b, s, axis=0), 0.0) + b
        a = a * jnp.where(sub >= s, pltpu.roll(a, s, axis=0), 1.0)
        s *= 2
    carry = hs_ref[...]
    groups = []
    for r0 in range(0, rows, SUBLANES):
        hg = b[r0:r0 + SUBLANES] + a[r0:r0 + SUBLANES] * carry
        groups.append(hg)
        carry = hg[SUBLANES - 1:SUBLANES, :]
    hseq = jnp.concatenate(groups, axis=0)
    hs_ref[...] = carry
    y_lru = _group_rms(jax.nn.gelu(gb) * hseq, lg_ref[...], ones_bd)

    cc = jnp.concatenate([ch_ref[...], cx], axis=0)
    conv = cc[HALO - 2:HALO - 2 + rows] * scw_ref[0:1, :]
    for k in range(1, 3):
        conv = conv + cc[HALO - 2 + k:HALO - 2 + k + rows] * scw_ref[k:k + 1, :]
    ch_ref[...] = cx[rows - HALO:, :]
    y_sc = _group_rms(sc_b * conv, scg_ref[...], ones_bd)

    o_ref[:, 0:w] = y_lru.astype(o_ref.dtype)
    o_ref[:, w:2 * w] = y_sc.astype(o_ref.dtype)


def _lrusc(h, params, batch, ones_bd):
    n, d = h.shape
    w = params[3].shape[1]
    nt = n // batch // LRU_ROWS
    row = lambda b, t: (b * nt + t, 0)
    in_specs = [pl.BlockSpec((LRU_ROWS, d), row)]
    in_specs += [_resident(p) for p in params]
    in_specs += [_resident(ones_bd)]
    return pl.pallas_call(
        _lrusc_kernel,
        grid=(batch, nt),
        in_specs=in_specs,
        out_specs=pl.BlockSpec((LRU_ROWS, 2 * w), row),
        out_shape=jax.ShapeDtypeStruct((n, 2 * w), BF16),
        scratch_shapes=[pltpu.VMEM((HALO, w), F32), pltpu.VMEM((HALO, w), F32),
                        pltpu.VMEM((1, w), F32)],
        compiler_params=_cparams(2),
        name="lru_shortconv",
    )(h, *[_operand(p) for p in params], ones_bd)


def _solve_unit_lower(lmats, rhss, same_blk, eye):
    dot = lambda a, b: jnp.dot(a, b, preferred_element_type=F32)
    lds = [jnp.where(same_blk, l, 0.0) for l in lmats]
    los = [_bf(l - ld) for l, ld in zip(lmats, lds)]
    ts = [eye + ld for ld in lds]
    tbs = [_bf(t) for t in ts]
    pbs = [_bf(ld) for ld in lds]
    s = 2
    while s < INV_BLOCK:
        pbs = [_bf(dot(pb, pb)) for pb in pbs]
        ts = [t + dot(tb, pb) for t, tb, pb in zip(ts, tbs, pbs)]
        tbs = [_bf(t) for t in ts]
        s *= 2
    xs = [dot(tb, _bf(r)) for tb, r in zip(tbs, rhss)]
    pbs = [_bf(dot(tb, lo)) for tb, lo in zip(tbs, los)]
    nb = CHUNK // INV_BLOCK
    s = 1
    while s < nb:
        xs = [x + dot(pb, _bf(x)) for x, pb in zip(xs, pbs)]
        s *= 2
        if s < nb:
            pbs = [_bf(dot(pb, pb)) for pb in pbs]
    return xs


def _rwkv_kernel(h_ref, gin_ref, win_ref, mu_ref, w0_ref, a0_ref, wl_ref, kk_ref, ka_ref, rk_ref,
                 lw_ref, lb_ref, bd_ref, o_ref, zl_ref, s_ref, p_ref, *, n_dl, n_il):
    width = w0_ref.shape[1]
    n_heads = width // HEAD
    n_slots = h_ref.shape[0]
    n_chunks = h_ref.shape[1] // CHUNK
    n_rows = n_slots * CHUNK

    @pl.when(pl.program_id(1) == 0)
    def _():
        zl_ref[...] = jnp.zeros_like(zl_ref)
        s_ref[...] = jnp.zeros_like(s_ref)

    n_a = win_ref.shape[1] - p_ref.shape[2]
    for slot in range(n_slots):
        p_ref[slot] = jnp.dot(_bf(_rms(h_ref[slot], gin_ref[...])), win_ref[:, n_a:],
                              preferred_element_type=F32)

    c2 = 2 * CHUNK
    ri = lax.broadcasted_iota(jnp.int32, (c2, c2), 0)
    ci = lax.broadcasted_iota(jnp.int32, (c2, c2), 1)
    cm = jnp.bitwise_and(ci, CHUNK - 1)
    gmask = ((ri < CHUNK) & (cm < ri)) | ((ri >= CHUNK) & (cm <= ri - CHUNK))
    ti = lax.broadcasted_iota(jnp.int32, (CHUNK, CHUNK), 0)
    si = lax.broadcasted_iota(jnp.int32, (CHUNK, CHUNK), 1)
    shift = INV_BLOCK.bit_length() - 1
    same_blk = jnp.right_shift(ti, shift) == jnp.right_shift(si, shift)
    eye = (ti == si).astype(F32)
    tri = (si <= ti).astype(BF16)
    ones_bd = bd_ref[...]
    bd_w = ones_bd.shape[0]

    def seg_sum(x):
        xb = _bf(x)
        parts = [jnp.dot(xb[:, c:c + bd_w], ones_bd, preferred_element_type=F32)
                 for c in range(0, x.shape[1], bd_w)]
        return jnp.concatenate(parts, axis=1)

    heads = range(n_heads)
    hsl = [slice(h * HEAD, (h + 1) * HEAD) for h in heads]
    slots = range(n_slots)
    stack = lambda xs: jnp.concatenate(xs, axis=0)
    unstack = lambda x: [x[s * CHUNK:(s + 1) * CHUNK] for s in slots]

    def chunk_body(c, carry):
        r0 = pl.multiple_of(c * CHUNK, CHUNK)
        zs = [p_ref[s, pl.ds(r0, CHUNK), :] for s in slots]
        first_row = lax.broadcasted_iota(jnp.int32, zs[0].shape, 0) < 1
        zz = []
        for s in slots:
            zprev = jnp.where(first_row, zl_ref[s], pltpu.roll(zs[s], 1, axis=0))
            zl_ref[s] = zs[s][CHUNK - 1:CHUNK, :]
            zz.append(zs[s] + (zprev - zs[s]) * mu_ref[...])
        zz = stack(zz)
        r = zz[:, 0:width]
        k = zz[:, width:2 * width]
        v = zz[:, 2 * width:3 * width]
        lo = zz[:, 3 * width:]
        lane = lax.broadcasted_iota(jnp.int32, lo.shape, 1)
        lo_act = jnp.where(lane < n_dl, jnp.tanh(lo),
                           jnp.where(lane < n_dl + n_il, lo, jax.nn.sigmoid(lo)))
        lora = _mm(lo_act, wl_ref[...])
        w_log = -_softplus(-(w0_ref[...] + lora[:, 0:width])) - 0.5
        lw = -jnp.exp(w_log)
        a = jax.nn.sigmoid(a0_ref[...] + lora[:, width:2 * width])
        g = lora[:, 2 * width:3 * width]
        kk = k * kk_ref[...]
        kf = k * (1.0 + (a - 1.0) * ka_ref[...])
        sums = seg_sum(jnp.concatenate([kk * kk, r * kf * rk_ref[...]], axis=0))
        kk = kk * lax.rsqrt(jnp.maximum(sums[0:n_rows], 1e-24))
        bonus = sums[n_rows:]
        kka = kk * a

        lw_hi = _bf(lw)
        lw_lo = _bf(lw - lw_hi.astype(F32))
        hi_l = jnp.concatenate(unstack(lw_hi), axis=1)
        lo_l = jnp.concatenate(unstack(lw_lo), axis=1)
        cum_l = (jnp.dot(tri, hi_l, preferred_element_type=F32)
                 + jnp.dot(tri, lo_l, preferred_element_type=F32))
        cum = stack([cum_l[:, s * width:(s + 1) * width] for s in slots])
        cl = stack([jnp.broadcast_to(cum_l[CHUNK - 1:CHUNK, s * width:(s + 1) * width],
                                     (CHUNK, width)) for s in slots])
        w_in = jnp.exp(-cum)
        w_rem = jnp.exp(cl - cum)
        al = unstack(_bf(-kk * jnp.exp(cum - lw)))
        be = unstack(_bf(kka * w_in))
        kt = unstack(_bf(kf * w_in))
        rt = unstack(_bf(r * jnp.exp(cum)))
        bw = unstack(_bf(kka * w_rem))
        kw = unstack(_bf(kf * w_rem))
        vb = unstack(_bf(v))
        w_c = jnp.exp(cum_l[CHUNK - 1:CHUNK, :])

        pairs = [(s, hs) for s in slots for hs in hsl]
        ars = [jnp.concatenate([al[s][:, hs], rt[s][:, hs]], axis=0) for s, hs in pairs]
        bks = [jnp.concatenate([be[s][:, hs], kt[s][:, hs]], axis=0) for s, hs in pairs]
        bkws = [jnp.concatenate([bw[s][:, hs], kw[s][:, hs]], axis=0) for s, hs in pairs]
        vhs = [vb[s][:, hs] for s, hs in pairs]
        wcs = [w_c[:, s * width + h * HEAD:s * width + (h + 1) * HEAD] for s in slots for h in heads]
        n_chains = n_slots * n_heads
        s0s = [s_ref[i] for i in range(n_chains)]
        gms = [jnp.where(gmask, _mm_nt(ar, bk), 0.0) for ar, bk in zip(ars, bks)]
        a_ss = [_mm_nt(ar, s0) for ar, s0 in zip(ars, s0s)]
        akvs = [_mm(gm[0:CHUNK, CHUNK:], v_h) for gm, v_h in zip(gms, vhs)]
        rhss = [a_s[0:CHUNK] + akv for a_s, akv in zip(a_ss, akvs)]
        es = _solve_unit_lower([gm[0:CHUNK, 0:CHUNK] for gm in gms], rhss, same_blk, eye)
        evs = [jnp.concatenate([_bf(e), v_h], axis=0) for e, v_h in zip(es, vhs)]
        ys = [a_s[CHUNK:] + _mm(gm[CHUNK:, :], ev) for a_s, gm, ev in zip(a_ss, gms, evs)]
        for i in range(n_chains):
            s_ref[i] = s0s[i] * wcs[i] + _mm_tn(evs[i], bkws[i])

        y = stack([jnp.concatenate(ys[s * n_heads:(s + 1) * n_heads], axis=1) for s in slots])
        mean = seg_sum(y) * (1.0 / HEAD)
        d = y - mean
        var = seg_sum(d * d) * (1.0 / HEAD)
        yn = d * lax.rsqrt(var + LNX_EPS) * lw_ref[...] + lb_ref[...]
        out = unstack(((yn + bonus * v) * g).astype(o_ref.dtype))
        for s in slots:
            o_ref[s, pl.ds(r0, CHUNK), :] = out[s]
        return carry

    lax.fori_loop(0, n_chunks, chunk_body, 0)


def _rwkv(h, params, batch, ones_bd, n_dl, n_il):
    n, d = h.shape
    nrw = params[2].shape[1]
    width = params[3].shape[1]
    t_pad = n // batch
    blk = lambda b, t: (b, t, 0)
    in_specs = [pl.BlockSpec((RWKV_SLOTS, RWKV_ROWS, d), blk)]
    in_specs += [_resident(p) for p in params]
    in_specs += [_resident(ones_bd)]
    y = pl.pallas_call(
        functools.partial(_rwkv_kernel, n_dl=n_dl, n_il=n_il),
        grid=(batch // RWKV_SLOTS, t_pad // RWKV_ROWS),
        in_specs=in_specs,
        out_specs=pl.BlockSpec((RWKV_SLOTS, RWKV_ROWS, width), blk),
        out_shape=jax.ShapeDtypeStruct((batch, t_pad, width), BF16),
        scratch_shapes=[pltpu.VMEM((RWKV_SLOTS, 1, nrw), F32),
                        pltpu.VMEM((RWKV_SLOTS * (width // HEAD), HEAD, HEAD), F32),
                        pltpu.VMEM((RWKV_SLOTS, RWKV_ROWS, nrw), F32)],
        compiler_params=_cparams(2),
        name="rwkv7",
    )(h.reshape(batch, t_pad, d), *[_operand(p) for p in params], ones_bd)
    return y.reshape(n, width)


def _ones_block_diag(width):
    return jnp.asarray(np.kron(np.eye(width // HEAD), np.ones((HEAD, HEAD))), BF16)


def _block_diag(w):
    g, i, j = w.shape
    return jnp.einsum('gij,gh->gihj', w, jnp.eye(g, dtype=w.dtype)).reshape(g * i, g * j)


def kernel(x, meta_tokens, norm_g, ffn1_w_in, ffn1_w_out, ffn2_w_in, ffn2_w_out, mix_w_in, mix_w_out, lru_conv_w, lru_conv_b, lru_wa, lru_ba, lru_wx, lru_bx, lru_lambda, lru_norm_g, sc_conv_w, sc_norm_g, rwkv_mu, rwkv_w0, rwkv_w2, rwkv_a0, rwkv_a2, rwkv_g2, rwkv_k_k, rwkv_k_a, rwkv_r_k, rwkv_lnx_w, rwkv_lnx_b):
    bsz, seq, d = x.shape
    n_meta = meta_tokens.shape[0]
    depth = norm_g.shape[0]
    lru_w = lru_conv_b.shape[1]
    sc_w = sc_norm_g.shape[1]
    rw_w = rwkv_w0.shape[1]
    assert lru_w == sc_w
    t_real = n_meta + seq
    t_pad = -(-t_real // TIME_ALIGN) * TIME_ALIGN

    meta = jnp.broadcast_to(meta_tokens.astype(x.dtype)[None], (bsz, n_meta, d))
    pad = jnp.zeros((bsz, t_pad - t_real, d), x.dtype)
    h = jnp.concatenate([meta, x, pad], axis=1).reshape(bsz * t_pad, d)

    ones_bd = _ones_block_diag(lru_w)
    row = lambda p: p.reshape(1, -1)
    f1_in, f1_out, f2_in, f2_out = _bf(ffn1_w_in), _bf(ffn1_w_out), _bf(ffn2_w_in), _bf(ffn2_w_out)
    m_in, m_out = _bf(mix_w_in), _bf(mix_w_out)

    for l in range(depth):
        g = norm_g[l]
        h = _ffn(h, row(g[0]), _Layer(f1_in, l), _Layer(f1_out, l), row(g[1]))

        w_in = _Layer(m_in, l)
        lru_params = (row(g[2]), w_in, lru_conv_w[l], row(lru_conv_b[l]),
                      _bf(_block_diag(lru_wa[l])), row(lru_ba[l]),
                      _bf(_block_diag(lru_wx[l])), row(lru_bx[l]),
                      row(lru_lambda[l]), row(lru_norm_g[l]), sc_conv_w[l], row(sc_norm_g[l]))
        y_a = _lrusc(h, lru_params, bsz, ones_bd)

        n_dl, n_il, n_gl = rwkv_w2.shape[1], rwkv_a2.shape[1], rwkv_g2.shape[1]
        lora_w = jnp.zeros((n_dl + n_il + n_gl, 3 * rw_w), F32)
        lora_w = lora_w.at[0:n_dl, 0:rw_w].set(rwkv_w2[l])
        lora_w = lora_w.at[n_dl:n_dl + n_il, rw_w:2 * rw_w].set(rwkv_a2[l])
        lora_w = lora_w.at[n_dl + n_il:, 2 * rw_w:].set(rwkv_g2[l])
        rw_params = (row(g[2]), w_in, row(rwkv_mu[l]), row(rwkv_w0[l]), row(rwkv_a0[l]),
                     _bf(lora_w), row(rwkv_k_k[l]), row(rwkv_k_a[l]), row(rwkv_r_k[l]),
                     row(rwkv_lnx_w[l]), row(rwkv_lnx_b[l]))
        y_rw = _rwkv(h, rw_params, bsz, ones_bd, n_dl, n_il)

        h = _ffn(h, row(g[4]), _Layer(f2_in, l), _Layer(f2_out, l), row(g[5]),
                 mix=(y_a, y_rw, _Layer(m_out, l), row(g[3])))

    return h.reshape(bsz, t_pad, d)[:, n_meta:t_real]
```

```python
import functools

import jax
import jax.numpy as jnp
import numpy as np
from jax import lax
from jax.experimental import pallas as pl
from jax.experimental.pallas import tpu as pltpu

F32 = jnp.float32
BF16 = jnp.bfloat16

RMS_EPS = 1e-6
LNX_EPS = 64e-5
LRU_C = 8.0

HEAD = 64
CHUNK = 64
INV_BLOCK = 16
RWKV_ROWS = 192
RWKV_SLOTS = 4
FFN_ROWS = 528
TIME_ALIGN = 2112
SUBLANES = 8
HALO = 8
VMEM_LIMIT = 56 * 1024 * 1024


def _cparams(n_axes):
    return pltpu.CompilerParams(
        dimension_semantics=("arbitrary",) * n_axes, vmem_limit_bytes=VMEM_LIMIT)


class _Layer:
    def __init__(self, stacked, layer):
        self.stacked, self.layer = stacked, layer
        self.shape = stacked.shape[1:]


def _operand(p):
    return p.stacked if isinstance(p, _Layer) else p


def _resident(p):
    zeros = (0,) * len(p.shape)
    if isinstance(p, _Layer):
        layer = p.layer
        return pl.BlockSpec((None,) + tuple(p.shape), lambda *_: (layer,) + zeros,
                            pipeline_mode=pl.Buffered(1))
    return pl.BlockSpec(p.shape, lambda *_: zeros, pipeline_mode=pl.Buffered(1))


def _rms(x, g):
    return x * lax.rsqrt(jnp.mean(x * x, axis=-1, keepdims=True) + RMS_EPS) * g


def _bf(x):
    return x.astype(BF16)


def _mm(a, b):
    return jnp.dot(_bf(a), _bf(b), preferred_element_type=F32)


def _mm_nt(a, b):
    return lax.dot_general(_bf(a), _bf(b), (((1,), (1,)), ((), ())), preferred_element_type=F32)


def _mm_tn(a, b):
    return lax.dot_general(_bf(a), _bf(b), (((0,), (0,)), ((), ())), preferred_element_type=F32)


def _softplus(x):
    return jnp.maximum(x, 0.0) + jnp.log1p(jnp.exp(-jnp.abs(x)))


def _ffn_kernel(*refs, has_mix):
    if has_mix:
        h_ref, ya_ref, yrw_ref, wm_ref, gm_ref, gpre_ref, wi_ref, wo_ref, gpost_ref, o_ref = refs
        na = ya_ref.shape[1]
        m = jnp.dot(ya_ref[...], wm_ref[:na, :], preferred_element_type=F32)
        m = m + jnp.dot(yrw_ref[...], wm_ref[na:, :], preferred_element_type=F32)
        h = h_ref[...] + _rms(m, gm_ref[...])
    else:
        h_ref, gpre_ref, wi_ref, wo_ref, gpost_ref, o_ref = refs
        h = h_ref[...]
    d_ff = wo_ref.shape[0]
    xn = _bf(_rms(h, gpre_ref[...]))
    gu = jnp.dot(xn, wi_ref[...], preferred_element_type=F32)
    act = _bf(jax.nn.silu(gu[:, :d_ff]) * gu[:, d_ff:])
    y = jnp.dot(act, wo_ref[...], preferred_element_type=F32)
    o_ref[...] = h + 0.5 * _rms(y, gpost_ref[...])


def _ffn(h, g_pre, w_in, w_out, g_post, mix=None):
    n, d = h.shape
    d_ff = w_out.shape[0]
    rows = lambda width: pl.BlockSpec((FFN_ROWS, width), lambda i: (i, 0))
    operands, in_specs = [h], [rows(d)]
    params = (g_pre, w_in, w_out, g_post)
    if mix is not None:
        y_a, y_rw, w_mix, g_mix = mix
        operands += [y_a, y_rw]
        in_specs += [rows(y_a.shape[1]), rows(y_rw.shape[1])]
        params = (w_mix, g_mix) + params
    operands += [_operand(p) for p in params]
    in_specs += [_resident(p) for p in params]
    return pl.pallas_call(
        functools.partial(_ffn_kernel, has_mix=mix is not None),
        grid=(n // FFN_ROWS,),
        in_specs=in_specs,
        out_specs=rows(d),
        out_shape=jax.ShapeDtypeStruct((n, d), F32),
        compiler_params=_cparams(1),
        name="ffn_mix" if mix is not None else "ffn",
    )(*operands)


def _group_rms(x, g, ones_bd):
    ms = _mm(x * x, ones_bd) * (1.0 / HEAD)
    return x * lax.rsqrt(ms + RMS_EPS) * g


def _lrusc_stages(p, prm, ones_bd, xh_ref, ch_ref, hs_ref, o_ref):
    cw_ref, cb_ref, wa_ref, ba_ref, wx_ref, bx_ref, lam_ref, lg_ref, scw_ref, scg_ref = prm
    w = cb_ref.shape[1]
    rows = p.shape[0]
    xb = p[:, 0:w]
    gb = p[:, w:2 * w]
    sc_b = p[:, 2 * w:3 * w]
    cx = p[:, 3 * w:4 * w] * p[:, 4 * w:5 * w]

    xx = jnp.concatenate([xh_ref[...], xb], axis=0)
    u = xx[HALO - 3:HALO - 3 + rows] * cw_ref[0:1, :]
    for k in range(1, 4):
        u = u + xx[HALO - 3 + k:HALO - 3 + k + rows] * cw_ref[k:k + 1, :]
    u = u + cb_ref[...]
    xh_ref[...] = xb[rows - HALO:, :]

    yield
    r = jax.nn.sigmoid(_mm(u, wa_ref[...]) + ba_ref[...])
    gi = jax.nn.sigmoid(_mm(u, wx_ref[...]) + bx_ref[...])
    log_a = -LRU_C * r * _softplus(-lam_ref[...])
    a = jnp.exp(log_a)
    th = jnp.tanh(log_a)
    b = jnp.sqrt(-2.0 * th / (1.0 - th)) * (gi * u)

    sub = jnp.bitwise_and(lax.broadcasted_iota(jnp.int32, a.shape, 0), SUBLANES - 1)
    s = 1
    while s < SUBLANES:
        b = a * jnp.where(sub >= s, pltpu.roll(b, s, axis=0), 0.0) + b
        a = a * jnp.where(sub >= s, pltpu.roll(a, s, axis=0), 1.0)
        s *= 2
    carry = hs_ref[...]
    groups = []
    for r0 in range(0, rows, SUBLANES):
        hg = b[r0:r0 + SUBLANES] + a[r0:r0 + SUBLANES] * carry
        groups.append(hg)
        carry = hg[SUBLANES - 1:SUBLANES, :]
    hseq = jnp.concatenate(groups, axis=0)
    hs_ref[...] = carry
    lru_out = jax.nn.gelu(gb) * hseq

    cc = jnp.concatenate([ch_ref[...], cx], axis=0)
    conv = cc[HALO - 2:HALO - 2 + rows] * scw_ref[0:1, :]
    for k in range(1, 3):
        conv = conv + cc[HALO - 2 + k:HALO - 2 + k + rows] * scw_ref[k:k + 1, :]
    ch_ref[...] = cx[rows - HALO:, :]

    yield
    o_ref[:, 0:w] = _group_rms(lru_out, lg_ref[...], ones_bd).astype(o_ref.dtype)
    o_ref[:, w:2 * w] = _group_rms(sc_b * conv, scg_ref[...], ones_bd).astype(o_ref.dtype)


def _solve_unit_lower(lmats, rhss, same_blk, eye):
    dot = lambda a, b: jnp.dot(a, b, preferred_element_type=F32)
    lds = [jnp.where(same_blk, l, 0.0) for l in lmats]
    los = [_bf(l - ld) for l, ld in zip(lmats, lds)]
    ts = [eye + ld for ld in lds]
    tbs = [_bf(t) for t in ts]
    pbs = [_bf(ld) for ld in lds]
    s = 2
    while s < INV_BLOCK:
        pbs = [_bf(dot(pb, pb)) for pb in pbs]
        ts = [t + dot(tb, pb) for t, tb, pb in zip(ts, tbs, pbs)]
        tbs = [_bf(t) for t in ts]
        s *= 2
    xs = [dot(tb, _bf(r)) for tb, r in zip(tbs, rhss)]
    pbs = [_bf(dot(tb, lo)) for tb, lo in zip(tbs, los)]
    nb = CHUNK // INV_BLOCK
    s = 1
    while s < nb:
        xs = [x + dot(pb, _bf(x)) for x, pb in zip(xs, pbs)]
        s *= 2
        if s < nb:
            pbs = [_bf(dot(pb, pb)) for pb in pbs]
    return xs


def _mixers_kernel(*refs, n_dl, n_il):
    (h_ref, gin_ref, win_ref), lru_prm, refs = refs[:3], refs[3:13], refs[13:]
    (mu_ref, w0_ref, a0_ref, wl_ref, kk_ref, ka_ref, rk_ref, lw_ref, lb_ref, bd_ref,
     ya_ref, o_ref, zl_ref, s_ref, p_ref, xh_ref, ch_ref, hs_ref) = refs
    width = w0_ref.shape[1]
    n_heads = width // HEAD
    n_slots = h_ref.shape[0]
    n_chunks = h_ref.shape[1] // CHUNK
    n_rows = n_slots * CHUNK
    ones_bd = bd_ref[...]

    @pl.when(pl.program_id(1) == 0)
    def _():
        for ref in (zl_ref, s_ref, xh_ref, ch_ref, hs_ref):
            ref[...] = jnp.zeros_like(ref)

    n_a = win_ref.shape[1] - p_ref.shape[2]
    pending = []
    for slot in range(n_slots):
        p = jnp.dot(_bf(_rms(h_ref[slot], gin_ref[...])), win_ref[...],
                    preferred_element_type=F32)
        p_ref[slot] = p[:, n_a:]
        pending.append(_lrusc_stages(p[:, 0:n_a], lru_prm, ones_bd, xh_ref.at[slot],
                                     ch_ref.at[slot], hs_ref.at[slot], ya_ref.at[slot]))
        pending = [g for g in pending if next(g, True) is None]
    while pending:
        pending = [g for g in pending if next(g, True) is None]

    c2 = 2 * CHUNK
    ri = lax.broadcasted_iota(jnp.int32, (c2, c2), 0)
    ci = lax.broadcasted_iota(jnp.int32, (c2, c2), 1)
    cm = jnp.bitwise_and(ci, CHUNK - 1)
    gmask = ((ri < CHUNK) & (cm < ri)) | ((ri >= CHUNK) & (cm <= ri - CHUNK))
    ti = lax.broadcasted_iota(jnp.int32, (CHUNK, CHUNK), 0)
    si = lax.broadcasted_iota(jnp.int32, (CHUNK, CHUNK), 1)
    shift = INV_BLOCK.bit_length() - 1
    same_blk = jnp.right_shift(ti, shift) == jnp.right_shift(si, shift)
    eye = (ti == si).astype(F32)
    tri = (si <= ti).astype(BF16)
    bd_w = ones_bd.shape[0]

    def seg_sum(x):
        xb = _bf(x)
        parts = [jnp.dot(xb[:, c:c + bd_w], ones_bd, preferred_element_type=F32)
                 for c in range(0, x.shape[1], bd_w)]
        return jnp.concatenate(parts, axis=1)

    heads = range(n_heads)
    hsl = [slice(h * HEAD, (h + 1) * HEAD) for h in heads]
    slots = range(n_slots)
    stack = lambda xs: jnp.concatenate(xs, axis=0)
    unstack = lambda x: [x[s * CHUNK:(s + 1) * CHUNK] for s in slots]

    def chunk_body(c, carry):
        r0 = pl.multiple_of(c * CHUNK, CHUNK)
        zs = [p_ref[s, pl.ds(r0, CHUNK), :] for s in slots]
        first_row = lax.broadcasted_iota(jnp.int32, zs[0].shape, 0) < 1
        zz = []
        for s in slots:
            zprev = jnp.where(first_row, zl_ref[s], pltpu.roll(zs[s], 1, axis=0))
            zl_ref[s] = zs[s][CHUNK - 1:CHUNK, :]
            zz.append(zs[s] + (zprev - zs[s]) * mu_ref[...])
        zz = stack(zz)
        r = zz[:, 0:width]
        k = zz[:, width:2 * width]
        v = zz[:, 2 * width:3 * width]
        lo = zz[:, 3 * width:]
        lane = lax.broadcasted_iota(jnp.int32, lo.shape, 1)
        lo_act = jnp.where(lane < n_dl, jnp.tanh(lo),
                           jnp.where(lane < n_dl + n_il, lo, jax.nn.sigmoid(lo)))
        lora = _mm(lo_act, wl_ref[...])
        w_log = -_softplus(-(w0_ref[...] + lora[:, 0:width])) - 0.5
        lw = -jnp.exp(w_log)
        a = jax.nn.sigmoid(a0_ref[...] + lora[:, width:2 * width])
        g = lora[:, 2 * width:3 * width]
        kk = k * kk_ref[...]
        kf = k * (1.0 + (a - 1.0) * ka_ref[...])
        sums = seg_sum(jnp.concatenate([kk * kk, r * kf * rk_ref[...]], axis=0))
        kk = kk * lax.rsqrt(jnp.maximum(sums[0:n_rows], 1e-24))
        bonus = sums[n_rows:]
        kka = kk * a

        lw_hi = _bf(lw)
        lw_lo = _bf(lw - lw_hi.astype(F32))
        hi_l = jnp.concatenate(unstack(lw_hi), axis=1)
        lo_l = jnp.concatenate(unstack(lw_lo), axis=1)
        cum_l = (jnp.dot(tri, hi_l, preferred_element_type=F32)
                 + jnp.dot(tri, lo_l, preferred_element_type=F32))
        cum = stack([cum_l[:, s * width:(s + 1) * width] for s in slots])
        cl = stack([jnp.broadcast_to(cum_l[CHUNK - 1:CHUNK, s * width:(s + 1) * width],
                                     (CHUNK, width)) for s in slots])
        w_in = jnp.exp(-cum)
        w_rem = jnp.exp(cl - cum)
        al = unstack(_bf(-kk * jnp.exp(cum - lw)))
        be = unstack(_bf(kka * w_in))
        kt = unstack(_bf(kf * w_in))
        rt = unstack(_bf(r * jnp.exp(cum)))
        bw = unstack(_bf(kka * w_rem))
        kw = unstack(_bf(kf * w_rem))
        vb = unstack(_bf(v))
        w_c = jnp.exp(cum_l[CHUNK - 1:CHUNK, :])

        pairs = [(s, hs) for s in slots for hs in hsl]
        ars = [jnp.concatenate([al[s][:, hs], rt[s][:, hs]], axis=0) for s, hs in pairs]
        bks = [jnp.concatenate([be[s][:, hs], kt[s][:, hs]], axis=0) for s, hs in pairs]
        bkws = [jnp.concatenate([bw[s][:, hs], kw[s][:, hs]], axis=0) for s, hs in pairs]
        vhs = [vb[s][:, hs] for s, hs in pairs]
        wcs = [w_c[:, s * width + h * HEAD:s * width + (h + 1) * HEAD] for s in slots for h in heads]
        n_chains = n_slots * n_heads
        s0s = [s_ref[i] for i in range(n_chains)]
        gms = [jnp.where(gmask, _mm_nt(ar, bk), 0.0) for ar, bk in zip(ars, bks)]
        a_ss = [_mm_nt(ar, s0) for ar, s0 in zip(ars, s0s)]
        akvs = [_mm(gm[0:CHUNK, CHUNK:], v_h) for gm, v_h in zip(gms, vhs)]
        rhss = [a_s[0:CHUNK] + akv for a_s, akv in zip(a_ss, akvs)]
        es = _solve_unit_lower([gm[0:CHUNK, 0:CHUNK] for gm in gms], rhss, same_blk, eye)
        evs = [jnp.concatenate([_bf(e), v_h], axis=0) for e, v_h in zip(es, vhs)]
        ys = [a_s[CHUNK:] + _mm(gm[CHUNK:, :], ev) for a_s, gm, ev in zip(a_ss, gms, evs)]
        for i in range(n_chains):
            s_ref[i] = s0s[i] * wcs[i] + _mm_tn(evs[i], bkws[i])

        y = stack([jnp.concatenate(ys[s * n_heads:(s + 1) * n_heads], axis=1) for s in slots])
        mean = seg_sum(y) * (1.0 / HEAD)
        d = y - mean
        var = seg_sum(d * d) * (1.0 / HEAD)
        yn = d * lax.rsqrt(var + LNX_EPS) * lw_ref[...] + lb_ref[...]
        out = unstack(((yn + bonus * v) * g).astype(o_ref.dtype))
        for s in slots:
            o_ref[s, pl.ds(r0, CHUNK), :] = out[s]
        return carry

    lax.fori_loop(0, n_chunks, chunk_body, 0)


def _mixers(h, g_in, w_in, lru_params, rw_params, batch, ones_bd, n_dl, n_il):
    n, d = h.shape
    nrw = rw_params[0].shape[1]
    width = rw_params[1].shape[1]
    w_a = lru_params[1].shape[1]
    t_pad = n // batch
    blk = lambda b, t: (b, t, 0)
    params = (g_in, w_in) + tuple(lru_params) + tuple(rw_params) + (ones_bd,)
    out_blk = lambda w: pl.BlockSpec((RWKV_SLOTS, RWKV_ROWS, w), blk)
    y_a, y_rw = pl.pallas_call(
        functools.partial(_mixers_kernel, n_dl=n_dl, n_il=n_il),
        grid=(batch // RWKV_SLOTS, t_pad // RWKV_ROWS),
        in_specs=[pl.BlockSpec((RWKV_SLOTS, RWKV_ROWS, d), blk)] + [_resident(p) for p in params],
        out_specs=[out_blk(2 * w_a), out_blk(width)],
        out_shape=[jax.ShapeDtypeStruct((batch, t_pad, 2 * w_a), BF16),
                   jax.ShapeDtypeStruct((batch, t_pad, width), BF16)],
        scratch_shapes=[pltpu.VMEM((RWKV_SLOTS, 1, nrw), F32),
                        pltpu.VMEM((RWKV_SLOTS * (width // HEAD), HEAD, HEAD), F32),
                        pltpu.VMEM((RWKV_SLOTS, RWKV_ROWS, nrw), F32),
                        pltpu.VMEM((RWKV_SLOTS, HALO, w_a), F32),
                        pltpu.VMEM((RWKV_SLOTS, HALO, w_a), F32),
                        pltpu.VMEM((RWKV_SLOTS, 1, w_a), F32)],
        compiler_params=_cparams(2),
        name="mixers",
    )(h.reshape(batch, t_pad, d), *[_operand(p) for p in params])
    return y_a.reshape(n, 2 * w_a), y_rw.reshape(n, width)


def _ones_block_diag(width):
    return jnp.asarray(np.kron(np.eye(width // HEAD), np.ones((HEAD, HEAD))), BF16)


def _block_diag(w):
    n, g, i, j = w.shape
    return jnp.einsum('lgij,gh->lgihj', w, jnp.eye(g, dtype=w.dtype)).reshape(n, g * i, g * j)


def kernel(x, meta_tokens, norm_g, ffn1_w_in, ffn1_w_out, ffn2_w_in, ffn2_w_out, mix_w_in, mix_w_out, lru_conv_w, lru_conv_b, lru_wa, lru_ba, lru_wx, lru_bx, lru_lambda, lru_norm_g, sc_conv_w, sc_norm_g, rwkv_mu, rwkv_w0, rwkv_w2, rwkv_a0, rwkv_a2, rwkv_g2, rwkv_k_k, rwkv_k_a, rwkv_r_k, rwkv_lnx_w, rwkv_lnx_b):
    bsz, seq, d = x.shape
    n_meta = meta_tokens.shape[0]
    depth = norm_g.shape[0]
    lru_w = lru_conv_b.shape[1]
    sc_w = sc_norm_g.shape[1]
    rw_w = rwkv_w0.shape[1]
    assert lru_w == sc_w
    t_real = n_meta + seq
    t_pad = -(-t_real // TIME_ALIGN) * TIME_ALIGN

    meta = jnp.broadcast_to(meta_tokens.astype(x.dtype)[None], (bsz, n_meta, d))
    pad = jnp.zeros((bsz, t_pad - t_real, d), x.dtype)
    h = jnp.concatenate([meta, x, pad], axis=1).reshape(bsz * t_pad, d)

    ones_bd = _ones_block_diag(lru_w)
    rows3 = lambda p: p.reshape(-1, 1, p.shape[-1])
    f1_in, f1_out, f2_in, f2_out = _bf(ffn1_w_in), _bf(ffn1_w_out), _bf(ffn2_w_in), _bf(ffn2_w_out)
    m_in, m_out = _bf(mix_w_in), _bf(mix_w_out)
    wa_bd, wx_bd = _bf(_block_diag(lru_wa)), _bf(_block_diag(lru_wx))
    n_dl, n_il, n_gl = rwkv_w2.shape[1], rwkv_a2.shape[1], rwkv_g2.shape[1]
    lora_w = jnp.zeros((depth, n_dl + n_il + n_gl, 3 * rw_w), F32)
    lora_w = lora_w.at[:, 0:n_dl, 0:rw_w].set(rwkv_w2)
    lora_w = lora_w.at[:, n_dl:n_dl + n_il, rw_w:2 * rw_w].set(rwkv_a2)
    lora_w = _bf(lora_w.at[:, n_dl + n_il:, 2 * rw_w:].set(rwkv_g2))
    norm_rows = rows3(norm_g)
    n_norms = norm_g.shape[1]
    lru_stacked = (lru_conv_w, rows3(lru_conv_b), wa_bd, rows3(lru_ba), wx_bd, rows3(lru_bx),
                   rows3(lru_lambda), rows3(lru_norm_g), sc_conv_w, rows3(sc_norm_g))
    rw_stacked = (rows3(rwkv_mu), rows3(rwkv_w0), rows3(rwkv_a0), lora_w, rows3(rwkv_k_k),
                  rows3(rwkv_k_a), rows3(rwkv_r_k), rows3(rwkv_lnx_w), rows3(rwkv_lnx_b))

    for l in range(depth):
        g = [_Layer(norm_rows, l * n_norms + i) for i in range(n_norms)]
        h = _ffn(h, g[0], _Layer(f1_in, l), _Layer(f1_out, l), g[1])
        y_a, y_rw = _mixers(h, g[2], _Layer(m_in, l), [_Layer(p, l) for p in lru_stacked],
                            [_Layer(p, l) for p in rw_stacked], bsz, ones_bd, n_dl, n_il)
        h = _ffn(h, g[4], _Layer(f2_in, l), _Layer(f2_out, l), g[5],
                 mix=(y_a, y_rw, _Layer(m_out, l), g[3]))

    return h.reshape(bsz, t_pad, d)[:, n_meta:t_real]
```

```python
import functools

import jax
import jax.numpy as jnp
import numpy as np
from jax import lax
from jax.experimental import pallas as pl
from jax.experimental.pallas import tpu as pltpu

F32 = jnp.float32
BF16 = jnp.bfloat16

RMS_EPS = 1e-6
LNX_EPS = 64e-5
LRU_C = 8.0

HEAD = 64
CHUNK = 64
INV_BLOCK = 16
RWKV_ROWS = 192
RWKV_SLOTS = 4
FFN_ROWS = 528
TIME_ALIGN = 2112
SUBLANES = 8
LANES = 128
PROJ_PIECES = 4
HALO = 8
VMEM_LIMIT = 56 * 1024 * 1024


def _cparams(n_axes):
    return pltpu.CompilerParams(
        dimension_semantics=("arbitrary",) * n_axes, vmem_limit_bytes=VMEM_LIMIT)


class _Layer:
    def __init__(self, stacked, layer):
        self.stacked, self.layer = stacked, layer
        self.shape = stacked.shape[1:]


def _operand(p):
    return p.stacked if isinstance(p, _Layer) else p


def _resident(p):
    zeros = (0,) * len(p.shape)
    if isinstance(p, _Layer):
        layer = p.layer
        return pl.BlockSpec((None,) + tuple(p.shape), lambda *_: (layer,) + zeros,
                            pipeline_mode=pl.Buffered(1))
    return pl.BlockSpec(p.shape, lambda *_: zeros, pipeline_mode=pl.Buffered(1))


def _rms(x, g):
    return x * lax.rsqrt(jnp.mean(x * x, axis=-1, keepdims=True) + RMS_EPS) * g


def _bf(x):
    return x.astype(BF16)


def _mm(a, b):
    return jnp.dot(_bf(a), _bf(b), preferred_element_type=F32)


def _mm_nt(a, b):
    return lax.dot_general(_bf(a), _bf(b), (((1,), (1,)), ((), ())), preferred_element_type=F32)


def _mm_tn(a, b):
    return lax.dot_general(_bf(a), _bf(b), (((0,), (0,)), ((), ())), preferred_element_type=F32)


def _softplus(x):
    return jnp.maximum(x, 0.0) + jnp.log1p(jnp.exp(-jnp.abs(x)))


def _ffn_kernel(*refs, has_mix):
    if has_mix:
        h_ref, ya_ref, yrw_ref, wm_ref, gm_ref, gpre_ref, wi_ref, wo_ref, gpost_ref, o_ref = refs
    else:
        h_ref, gpre_ref, wi_ref, wo_ref, gpost_ref, o_ref = refs
    h = h_ref[...]
    if has_mix:
        na = ya_ref.shape[1]
        m = jnp.dot(ya_ref[...], wm_ref[:na, :], preferred_element_type=F32)
        m = m + jnp.dot(yrw_ref[...], wm_ref[na:, :], preferred_element_type=F32)
        h = h + _rms(m, gm_ref[...])
    d_ff = wo_ref.shape[0]
    xn = _bf(_rms(h, gpre_ref[...]))
    gu = jnp.dot(xn, wi_ref[...], preferred_element_type=F32)
    act = _bf(jax.nn.silu(gu[:, :d_ff]) * gu[:, d_ff:])
    y = jnp.dot(act, wo_ref[...], preferred_element_type=F32)
    o_ref[...] = h + 0.5 * _rms(y, gpost_ref[...])


def _ffn(h, g_pre, w_in, w_out, g_post, mix=None):
    n, d = h.shape
    d_ff = w_out.shape[0]
    rows = lambda width: pl.BlockSpec((FFN_ROWS, width), lambda i: (i, 0))
    operands, in_specs = [h], [rows(d)]
    params = (g_pre, w_in, w_out, g_post)
    if mix is not None:
        y_a, y_rw, w_mix, g_mix = mix
        operands += [y_a, y_rw]
        in_specs += [rows(y_a.shape[1]), rows(y_rw.shape[1])]
        params = (w_mix, g_mix) + params
    operands += [_operand(p) for p in params]
    in_specs += [_resident(p) for p in params]
    return pl.pallas_call(
        functools.partial(_ffn_kernel, has_mix=mix is not None),
        grid=(n // FFN_ROWS,),
        in_specs=in_specs,
        out_specs=rows(d),
        out_shape=jax.ShapeDtypeStruct((n, d), F32),
        compiler_params=_cparams(1),
        name="ffn_mix" if mix is not None else "ffn",
    )(*operands)


def _group_rms(x, g, ones_bd):
    ms = _mm(x * x, ones_bd) * (1.0 / HEAD)
    return x * lax.rsqrt(ms + RMS_EPS) * g


def _lrusc_stages(p, prm, ones_bd, xh_ref, ch_ref, hs_ref, o_ref):
    cw_ref, cb_ref, wa_ref, ba_ref, wx_ref, bx_ref, lam_ref, lg_ref, scw_ref, scg_ref = prm
    w = cb_ref.shape[1]
    rows = p.shape[0]
    xb = p[:, 0:w]
    gb = p[:, w:2 * w]
    sc_b = p[:, 2 * w:3 * w]
    cx = p[:, 3 * w:4 * w] * p[:, 4 * w:5 * w]

    xx = jnp.concatenate([xh_ref[...], xb], axis=0)
    u = xx[HALO - 3:HALO - 3 + rows] * cw_ref[0:1, :]
    for k in range(1, 4):
        u = u + xx[HALO - 3 + k:HALO - 3 + k + rows] * cw_ref[k:k + 1, :]
    u = u + cb_ref[...]
    xh_ref[...] = xb[rows - HALO:, :]

    yield
    r = jax.nn.sigmoid(_mm(u, wa_ref[...]) + ba_ref[...])
    gi = jax.nn.sigmoid(_mm(u, wx_ref[...]) + bx_ref[...])
    log_a = -LRU_C * r * _softplus(-lam_ref[...])
    a = jnp.exp(log_a)
    th = jnp.tanh(log_a)
    b = jnp.sqrt(-2.0 * th / (1.0 - th)) * (gi * u)

    sub = jnp.bitwise_and(lax.broadcasted_iota(jnp.int32, a.shape, 0), SUBLANES - 1)
    s = 1
    while s < SUBLANES:
        b = a * jnp.where(sub >= s, pltpu.roll(b, s, axis=0), 0.0) + b
        a = a * jnp.where(sub >= s, pltpu.roll(a, s, axis=0), 1.0)
        s *= 2
    carry = hs_ref[...]
    groups = []
    for r0 in range(0, rows, SUBLANES):
        hg = b[r0:r0 + SUBLANES] + a[r0:r0 + SUBLANES] * carry
        groups.append(hg)
        carry = hg[SUBLANES - 1:SUBLANES, :]
    hseq = jnp.concatenate(groups, axis=0)
    hs_ref[...] = carry
    lru_out = jax.nn.gelu(gb) * hseq

    cc = jnp.concatenate([ch_ref[...], cx], axis=0)
    conv = cc[HALO - 2:HALO - 2 + rows] * scw_ref[0:1, :]
    for k in range(1, 3):
        conv = conv + cc[HALO - 2 + k:HALO - 2 + k + rows] * scw_ref[k:k + 1, :]
    ch_ref[...] = cx[rows - HALO:, :]

    yield
    o_ref[:, 0:w] = _group_rms(lru_out, lg_ref[...], ones_bd).astype(o_ref.dtype)
    o_ref[:, w:2 * w] = _group_rms(sc_b * conv, scg_ref[...], ones_bd).astype(o_ref.dtype)


def _pair_bd(x):
    first = lax.broadcasted_iota(jnp.int32, x.shape, 1) < HEAD
    zero = jnp.zeros_like(x)
    return jnp.concatenate([jnp.where(first, x, zero), jnp.where(first, zero, x)], axis=0)


def _solve_unit_lower(lmats, rhss, same_blk, eye):
    dot = lambda a, b: jnp.dot(a, _pair_bd(b), preferred_element_type=F32)
    lds = [jnp.where(same_blk, l, 0.0) for l in lmats]
    los = [_bf(l - ld) for l, ld in zip(lmats, lds)]
    ts = [eye + ld for ld in lds]
    tbs = [_bf(t) for t in ts]
    pbs = [_bf(ld) for ld in lds]
    s = 2
    while s < INV_BLOCK:
        pbs = [_bf(dot(pb, pb)) for pb in pbs]
        ts = [t + dot(tb, pb) for t, tb, pb in zip(ts, tbs, pbs)]
        tbs = [_bf(t) for t in ts]
        s *= 2
    xs = [dot(tb, _bf(r)) for tb, r in zip(tbs, rhss)]
    pbs = [_bf(dot(tb, lo)) for tb, lo in zip(tbs, los)]
    nb = CHUNK // INV_BLOCK
    s = 1
    while s < nb:
        xs = [x + dot(pb, _bf(x)) for x, pb in zip(xs, pbs)]
        s *= 2
        if s < nb:
            pbs = [_bf(dot(pb, pb)) for pb in pbs]
    return xs


def _mixers_kernel(*refs, n_dl, n_il):
    (h_ref, gin_ref, win_ref), lru_prm, refs = refs[:3], refs[3:13], refs[13:]
    (mu_ref, w0_ref, a0_ref, wl_ref, kk_ref, ka_ref, rk_ref, lw_ref, lb_ref, bd_ref,
     ya_ref, o_ref, zl_ref, s_ref, p_ref, xh_ref, ch_ref, hs_ref) = refs
    width = w0_ref.shape[1]
    n_heads = width // HEAD
    n_slots = h_ref.shape[0]
    n_chunks = h_ref.shape[1] // CHUNK
    n_rows = n_slots * CHUNK
    ones_bd = bd_ref[...]

    @pl.when(pl.program_id(1) == 0)
    def _():
        for ref in (zl_ref, s_ref, xh_ref, ch_ref, hs_ref):
            ref[...] = jnp.zeros_like(ref)

    t_rows = h_ref.shape[1]
    n_rw = p_ref.shape[2]
    n_a = win_ref.shape[1] - n_rw
    xn = _bf(_rms(h_ref[...].reshape(n_slots * t_rows, h_ref.shape[2]), gin_ref[...]))
    p_a = jnp.dot(xn, win_ref[:, 0:n_a], preferred_element_type=F32)
    pending = [_lrusc_stages(p_a[s * t_rows:(s + 1) * t_rows], lru_prm, ones_bd, xh_ref.at[s],
                             ch_ref.at[s], hs_ref.at[s], ya_ref.at[s]) for s in range(n_slots)]
    piece = -(-n_rw // (PROJ_PIECES * LANES)) * LANES
    for c0 in range(0, n_rw, piece):
        c1 = min(c0 + piece, n_rw)
        p_rw = jnp.dot(xn, win_ref[:, n_a + c0:n_a + c1], preferred_element_type=F32)
        for s in range(n_slots):
            p_ref[s, :, c0:c1] = p_rw[s * t_rows:(s + 1) * t_rows]
        pending = [g for g in pending if next(g, True) is None]
    while pending:
        pending = [g for g in pending if next(g, True) is None]

    pair = 2 * HEAD
    n_pairs = width // pair
    c2 = 2 * CHUNK
    ri = lax.broadcasted_iota(jnp.int32, (c2, 2 * pair), 0)
    cm = jnp.bitwise_and(lax.broadcasted_iota(jnp.int32, (c2, 2 * pair), 1), CHUNK - 1)
    gmask = ((ri < CHUNK) & (cm < ri)) | ((ri >= CHUNK) & (cm <= ri - CHUNK))
    ti = lax.broadcasted_iota(jnp.int32, (CHUNK, pair), 0)
    si = jnp.bitwise_and(lax.broadcasted_iota(jnp.int32, (CHUNK, pair), 1), CHUNK - 1)
    shift = INV_BLOCK.bit_length() - 1
    same_blk = jnp.right_shift(ti, shift) == jnp.right_shift(si, shift)
    eye = (ti == si).astype(F32)
    ti = lax.broadcasted_iota(jnp.int32, (CHUNK, CHUNK), 0)
    tri = (lax.broadcasted_iota(jnp.int32, (CHUNK, CHUNK), 1) <= ti).astype(BF16)
    same_head = ((lax.broadcasted_iota(jnp.int32, (pair, pair), 0) < HEAD)
                 == (lax.broadcasted_iota(jnp.int32, (pair, pair), 1) < HEAD))
    bd_w = ones_bd.shape[0]

    def seg_sum(x):
        xb = _bf(x)
        parts = [jnp.dot(xb[:, c:c + bd_w], ones_bd, preferred_element_type=F32)
                 for c in range(0, x.shape[1], bd_w)]
        return jnp.concatenate(parts, axis=1)

    slots = range(n_slots)
    stack = lambda xs: jnp.concatenate(xs, axis=0)
    unstack = lambda x: [x[s * CHUNK:(s + 1) * CHUNK] for s in slots]

    def chunk_body(c, carry):
        r0 = pl.multiple_of(c * CHUNK, CHUNK)
        zs = [p_ref[s, pl.ds(r0, CHUNK), :] for s in slots]
        first_row = lax.broadcasted_iota(jnp.int32, zs[0].shape, 0) < 1
        zz = []
        for s in slots:
            zprev = jnp.where(first_row, zl_ref[s], pltpu.roll(zs[s], 1, axis=0))
            zl_ref[s] = zs[s][CHUNK - 1:CHUNK, :]
            zz.append(zs[s] + (zprev - zs[s]) * mu_ref[...])
        zz = stack(zz)
        r = zz[:, 0:width]
        k = zz[:, width:2 * width]
        v = zz[:, 2 * width:3 * width]
        lo = zz[:, 3 * width:]
        lane = lax.broadcasted_iota(jnp.int32, lo.shape, 1)
        lo_act = jnp.where(lane < n_dl, jnp.tanh(lo),
                           jnp.where(lane < n_dl + n_il, lo, jax.nn.sigmoid(lo)))
        lora = _mm(lo_act, wl_ref[...])
        w_log = -_softplus(-(w0_ref[...] + lora[:, 0:width])) - 0.5
        lw = -jnp.exp(w_log)
        a = jax.nn.sigmoid(a0_ref[...] + lora[:, width:2 * width])
        g = lora[:, 2 * width:3 * width]
        kk = k * kk_ref[...]
        kf = k * (1.0 + (a - 1.0) * ka_ref[...])
        sums = seg_sum(jnp.concatenate([kk * kk, r * kf * rk_ref[...]], axis=0))
        kk = kk * lax.rsqrt(jnp.maximum(sums[0:n_rows], 1e-24))
        bonus = sums[n_rows:]
        kka = kk * a

        lw_hi = _bf(lw)
        lw_lo = _bf(lw - lw_hi.astype(F32))
        hi_l = jnp.concatenate(unstack(lw_hi), axis=1)
        lo_l = jnp.concatenate(unstack(lw_lo), axis=1)
        cum_l = (jnp.dot(tri, hi_l, preferred_element_type=F32)
                 + jnp.dot(tri, lo_l, preferred_element_type=F32))
        cum = stack([cum_l[:, s * width:(s + 1) * width] for s in slots])
        cl = stack([jnp.broadcast_to(cum_l[CHUNK - 1:CHUNK, s * width:(s + 1) * width],
                                     (CHUNK, width)) for s in slots])
        w_in = jnp.exp(-cum)
        w_rem = jnp.exp(cl - cum)
        al = unstack(_bf(-kk * jnp.exp(cum - lw)))
        be = unstack(_bf(kka * w_in))
        kt = unstack(_bf(kf * w_in))
        rt = unstack(_bf(r * jnp.exp(cum)))
        bw = unstack(_bf(kka * w_rem))
        kw = unstack(_bf(kf * w_rem))
        vb = unstack(_bf(v))
        w_c = jnp.exp(cum_l[CHUNK - 1:CHUNK, :])

        chains = [(s, slice(j * pair, (j + 1) * pair)) for s in slots for j in range(n_pairs)]
        cat = jnp.concatenate
        ars = [cat([al[s][:, ps], rt[s][:, ps]], axis=0) for s, ps in chains]
        bks = [cat([_pair_bd(be[s][:, ps]), _pair_bd(kt[s][:, ps])], axis=0) for s, ps in chains]
        bkws = [cat([bw[s][:, ps], kw[s][:, ps]], axis=0) for s, ps in chains]
        vps = [vb[s][:, ps] for s, ps in chains]
        vbds = [_pair_bd(v_p) for v_p in vps]
        wcs = [w_c[:, s * width + j * pair:s * width + (j + 1) * pair]
               for s in slots for j in range(n_pairs)]
        n_chains = n_slots * n_pairs
        s0s = [s_ref[i] for i in range(n_chains)]
        gms = [jnp.where(gmask, _mm_nt(ar, bk), 0.0) for ar, bk in zip(ars, bks)]
        a_ss = [_mm_nt(ar, s0) for ar, s0 in zip(ars, s0s)]
        akvs = [_mm(gm[0:CHUNK, pair:], vbd) for gm, vbd in zip(gms, vbds)]
        rhss = [a_s[0:CHUNK] + akv for a_s, akv in zip(a_ss, akvs)]
        es = _solve_unit_lower([gm[0:CHUNK, 0:pair] for gm in gms], rhss, same_blk, eye)
        ebs = [_bf(e) for e in es]
        ys = [a_s[CHUNK:] + _mm(gm[CHUNK:, :], cat([_pair_bd(eb), vbd], axis=0))
              for a_s, gm, eb, vbd in zip(a_ss, gms, ebs, vbds)]
        for i in range(n_chains):
            upd = _mm_tn(cat([ebs[i], vps[i]], axis=0), bkws[i])
            s_ref[i] = s0s[i] * wcs[i] + jnp.where(same_head, upd, 0.0)

        y = stack([cat(ys[s * n_pairs:(s + 1) * n_pairs], axis=1) for s in slots])
        mean = seg_sum(y) * (1.0 / HEAD)
        d = y - mean
        var = seg_sum(d * d) * (1.0 / HEAD)
        yn = d * lax.rsqrt(var + LNX_EPS) * lw_ref[...] + lb_ref[...]
        out = unstack(((yn + bonus * v) * g).astype(o_ref.dtype))
        for s in slots:
            o_ref[s, pl.ds(r0, CHUNK), :] = out[s]
        return carry

    lax.fori_loop(0, n_chunks, chunk_body, 0)


def _mixers(h, g_in, w_in, lru_params, rw_params, batch, ones_bd, n_dl, n_il):
    n, d = h.shape
    nrw = rw_params[0].shape[1]
    width = rw_params[1].shape[1]
    w_a = lru_params[1].shape[1]
    t_pad = n // batch
    blk = lambda b, t: (b, t, 0)
    params = (g_in, w_in) + tuple(lru_params) + tuple(rw_params) + (ones_bd,)
    out_blk = lambda w: pl.BlockSpec((RWKV_SLOTS, RWKV_ROWS, w), blk)
    y_a, y_rw = pl.pallas_call(
        functools.partial(_mixers_kernel, n_dl=n_dl, n_il=n_il),
        grid=(batch // RWKV_SLOTS, t_pad // RWKV_ROWS),
        in_specs=[pl.BlockSpec((RWKV_SLOTS, RWKV_ROWS, d), blk)] + [_resident(p) for p in params],
        out_specs=[out_blk(2 * w_a), out_blk(width)],
        out_shape=[jax.ShapeDtypeStruct((batch, t_pad, 2 * w_a), BF16),
                   jax.ShapeDtypeStruct((batch, t_pad, width), BF16)],
        scratch_shapes=[pltpu.VMEM((RWKV_SLOTS, 1, nrw), F32),
                        pltpu.VMEM((RWKV_SLOTS * (width // (2 * HEAD)), 2 * HEAD, 2 * HEAD), F32),
                        pltpu.VMEM((RWKV_SLOTS, RWKV_ROWS, nrw), F32),
                        pltpu.VMEM((RWKV_SLOTS, HALO, w_a), F32),
                        pltpu.VMEM((RWKV_SLOTS, HALO, w_a), F32),
                        pltpu.VMEM((RWKV_SLOTS, 1, w_a), F32)],
        compiler_params=_cparams(2),
        name="mixers",
    )(h.reshape(batch, t_pad, d), *[_operand(p) for p in params])
    return y_a.reshape(n, 2 * w_a), y_rw.reshape(n, width)


def _ones_block_diag(width):
    return jnp.asarray(np.kron(np.eye(width // HEAD), np.ones((HEAD, HEAD))), BF16)


def _block_diag(w):
    n, g, i, j = w.shape
    return jnp.einsum('lgij,gh->lgihj', w, jnp.eye(g, dtype=w.dtype)).reshape(n, g * i, g * j)


def kernel(x, meta_tokens, norm_g, ffn1_w_in, ffn1_w_out, ffn2_w_in, ffn2_w_out, mix_w_in, mix_w_out, lru_conv_w, lru_conv_b, lru_wa, lru_ba, lru_wx, lru_bx, lru_lambda, lru_norm_g, sc_conv_w, sc_norm_g, rwkv_mu, rwkv_w0, rwkv_w2, rwkv_a0, rwkv_a2, rwkv_g2, rwkv_k_k, rwkv_k_a, rwkv_r_k, rwkv_lnx_w, rwkv_lnx_b):
    bsz, seq, d = x.shape
    n_meta = meta_tokens.shape[0]
    depth = norm_g.shape[0]
    lru_w = lru_conv_b.shape[1]
    sc_w = sc_norm_g.shape[1]
    rw_w = rwkv_w0.shape[1]
    assert lru_w == sc_w
    t_real = n_meta + seq
    t_pad = -(-t_real // TIME_ALIGN) * TIME_ALIGN

    meta = jnp.broadcast_to(meta_tokens.astype(x.dtype)[None], (bsz, n_meta, d))
    pad = jnp.zeros((bsz, t_pad - t_real, d), x.dtype)
    h = jnp.concatenate([meta, x, pad], axis=1).reshape(bsz * t_pad, d)

    ones_bd = _ones_block_diag(lru_w)
    rows3 = lambda p: p.reshape(-1, 1, p.shape[-1])
    f1_in, f1_out, f2_in, f2_out = _bf(ffn1_w_in), _bf(ffn1_w_out), _bf(ffn2_w_in), _bf(ffn2_w_out)
    m_in, m_out = _bf(mix_w_in), _bf(mix_w_out)
    wa_bd, wx_bd = _bf(_block_diag(lru_wa)), _bf(_block_diag(lru_wx))
    n_dl, n_il, n_gl = rwkv_w2.shape[1], rwkv_a2.shape[1], rwkv_g2.shape[1]
    lora_w = jnp.zeros((depth, n_dl + n_il + n_gl, 3 * rw_w), F32)
    lora_w = lora_w.at[:, 0:n_dl, 0:rw_w].set(rwkv_w2)
    lora_w = lora_w.at[:, n_dl:n_dl + n_il, rw_w:2 * rw_w].set(rwkv_a2)
    lora_w = _bf(lora_w.at[:, n_dl + n_il:, 2 * rw_w:].set(rwkv_g2))
    norm_rows = rows3(norm_g)
    n_norms = norm_g.shape[1]
    lru_stacked = (lru_conv_w, rows3(lru_conv_b), wa_bd, rows3(lru_ba), wx_bd, rows3(lru_bx),
                   rows3(lru_lambda), rows3(lru_norm_g), sc_conv_w, rows3(sc_norm_g))
    rw_stacked = (rows3(rwkv_mu), rows3(rwkv_w0), rows3(rwkv_a0), lora_w, rows3(rwkv_k_k),
                  rows3(rwkv_k_a), rows3(rwkv_r_k), rows3(rwkv_lnx_w), rows3(rwkv_lnx_b))

    for l in range(depth):
        g = [_Layer(norm_rows, l * n_norms + i) for i in range(n_norms)]
        h = _ffn(h, g[0], _Layer(f1_in, l), _Layer(f1_out, l), g[1])
        y_a, y_rw = _mixers(h, g[2], _Layer(m_in, l), [_Layer(p, l) for p in lru_stacked],
                            [_Layer(p, l) for p in rw_stacked], bsz, ones_bd, n_dl, n_il)
        h = _ffn(h, g[4], _Layer(f2_in, l), _Layer(f2_out, l), g[5],
                 mix=(y_a, y_rw, _Layer(m_out, l), g[3]))

    return h.reshape(bsz, t_pad, d)[:, n_meta:t_real]
```

```python
import functools
import math

import jax
import jax.numpy as jnp
import numpy as np
from jax import lax
from jax.experimental import pallas as pl
from jax.experimental.pallas import tpu as pltpu

F32 = jnp.float32
BF16 = jnp.bfloat16

RMS_EPS = 1e-6
LNX_EPS = 64e-5
LRU_C = 8.0

HEAD = 64
CHUNK = 64
INV_BLOCK = 16
RWKV_ROWS = 192
RWKV_SLOTS = 4
FFN_ROWS = 528
OUT_ROWS = 512
TIME_ALIGN = 2112
SUBLANES = 8
LANES = 128
PROJ_PIECES = 4
HALO = 8
VMEM_LIMIT = 56 * 1024 * 1024


def _cparams(n_axes):
    return pltpu.CompilerParams(
        dimension_semantics=("arbitrary",) * n_axes, vmem_limit_bytes=VMEM_LIMIT)


class _Layer:
    def __init__(self, stacked, layer):
        self.stacked, self.layer = stacked, layer
        self.shape = stacked.shape[1:]


def _operand(p):
    return p.stacked if isinstance(p, _Layer) else p


def _resident(p):
    zeros = (0,) * len(p.shape)
    if isinstance(p, _Layer):
        layer = p.layer
        return pl.BlockSpec((None,) + tuple(p.shape), lambda *_: (layer,) + zeros,
                            pipeline_mode=pl.Buffered(1))
    return pl.BlockSpec(p.shape, lambda *_: zeros, pipeline_mode=pl.Buffered(1))


def _rms(x, g):
    return x * lax.rsqrt(jnp.mean(x * x, axis=-1, keepdims=True) + RMS_EPS) * g


def _bf(x):
    return x.astype(BF16)


def _mm(a, b):
    return jnp.dot(_bf(a), _bf(b), preferred_element_type=F32)


def _mm_nt(a, b):
    return lax.dot_general(_bf(a), _bf(b), (((1,), (1,)), ((), ())), preferred_element_type=F32)


def _mm_tn(a, b):
    return lax.dot_general(_bf(a), _bf(b), (((0,), (0,)), ((), ())), preferred_element_type=F32)


def _softplus(x):
    return jnp.maximum(x, 0.0) + jnp.log(1.0 + jnp.exp(-jnp.abs(x)))


def _ffn_kernel(*refs, has_mix):
    if has_mix:
        h_ref, ya_ref, yrw_ref, wm_ref, gm_ref, gpre_ref, wi_ref, wo_ref, gpost_ref, o_ref = refs
    else:
        h_ref, gpre_ref, wi_ref, wo_ref, gpost_ref, o_ref = refs
    h = h_ref[...]
    if has_mix:
        na = ya_ref.shape[1]
        m = jnp.dot(ya_ref[...], wm_ref[:na, :], preferred_element_type=F32)
        m = m + jnp.dot(yrw_ref[...], wm_ref[na:, :], preferred_element_type=F32)
        h = h + _rms(m, gm_ref[...])
    d_ff = wo_ref.shape[0]
    xn = _bf(_rms(h, gpre_ref[...]))
    gu = jnp.dot(xn, wi_ref[...], preferred_element_type=F32)
    act = _bf(jax.nn.silu(gu[:, :d_ff]) * gu[:, d_ff:])
    y = jnp.dot(act, wo_ref[...], preferred_element_type=F32)
    o_ref[...] = h + 0.5 * _rms(y, gpost_ref[...])


def _ffn(h, g_pre, w_in, w_out, g_post, mix=None, gather=None):
    n, d = h.shape
    d_ff = w_out.shape[0]
    if gather is None:
        block_rows, n_blocks = FFN_ROWS, n // FFN_ROWS
        rows = lambda width: pl.BlockSpec((block_rows, width), lambda i: (i, 0))
    else:
        block_rows, n_blocks, first_row = gather
        rows = lambda width: pl.BlockSpec((pl.Element(block_rows), pl.Element(width)),
                                          lambda i: (first_row(i), 0))
    operands, in_specs = [h], [rows(d)]
    params = (g_pre, w_in, w_out, g_post)
    if mix is not None:
        y_a, y_rw, w_mix, g_mix = mix
        operands += [y_a, y_rw]
        in_specs += [rows(y_a.shape[1]), rows(y_rw.shape[1])]
        params = (w_mix, g_mix) + params
    operands += [_operand(p) for p in params]
    in_specs += [_resident(p) for p in params]
    return pl.pallas_call(
        functools.partial(_ffn_kernel, has_mix=mix is not None),
        grid=(n_blocks,),
        in_specs=in_specs,
        out_specs=pl.BlockSpec((block_rows, d), lambda i: (i, 0)),
        out_shape=jax.ShapeDtypeStruct((n_blocks * block_rows, d), F32),
        compiler_params=_cparams(1),
        name="ffn_mix" if mix is not None else "ffn",
    )(*operands)


def _group_rms(x, g, ones_bd):
    ms = _mm(x * x, ones_bd) * (1.0 / HEAD)
    return x * lax.rsqrt(ms + RMS_EPS) * g


def _lrusc_stages(p, prm, ones_bd, xh_ref, ch_ref, hs_ref, o_ref):
    cw_ref, cb_ref, wa_ref, ba_ref, wx_ref, bx_ref, lam_ref, lg_ref, scw_ref, scg_ref = prm
    w = cb_ref.shape[1]
    rows = p.shape[0]
    xb = p[:, 0:w]
    gb = p[:, w:2 * w]
    sc_b = p[:, 2 * w:3 * w]
    cx = p[:, 3 * w:4 * w] * p[:, 4 * w:5 * w]

    xx = jnp.concatenate([xh_ref[...], xb], axis=0)
    u = xx[HALO - 3:HALO - 3 + rows] * cw_ref[0:1, :]
    for k in range(1, 4):
        u = u + xx[HALO - 3 + k:HALO - 3 + k + rows] * cw_ref[k:k + 1, :]
    u = u + cb_ref[...]
    xh_ref[...] = xb[rows - HALO:, :]

    yield
    r = jax.nn.sigmoid(_mm(u, wa_ref[...]) + ba_ref[...])
    gi = jax.nn.sigmoid(_mm(u, wx_ref[...]) + bx_ref[...])
    log_a = -LRU_C * r * _softplus(-lam_ref[...])
    a = jnp.exp(log_a)
    th = jnp.tanh(log_a)
    b = jnp.sqrt(-2.0 * th / (1.0 - th)) * (gi * u)

    sub = jnp.bitwise_and(lax.broadcasted_iota(jnp.int32, a.shape, 0), SUBLANES - 1)
    s = 1
    while s < SUBLANES:
        b = a * jnp.where(sub >= s, pltpu.roll(b, s, axis=0), 0.0) + b
        a = a * jnp.where(sub >= s, pltpu.roll(a, s, axis=0), 1.0)
        s *= 2
    carry = hs_ref[...]
    groups = []
    for r0 in range(0, rows, SUBLANES):
        hg = b[r0:r0 + SUBLANES] + a[r0:r0 + SUBLANES] * carry
        groups.append(hg)
        carry = hg[SUBLANES - 1:SUBLANES, :]
    hseq = jnp.concatenate(groups, axis=0)
    hs_ref[...] = carry
    lru_out = jax.nn.gelu(gb) * hseq

    cc = jnp.concatenate([ch_ref[...], cx], axis=0)
    conv = cc[HALO - 2:HALO - 2 + rows] * scw_ref[0:1, :]
    for k in range(1, 3):
        conv = conv + cc[HALO - 2 + k:HALO - 2 + k + rows] * scw_ref[k:k + 1, :]
    ch_ref[...] = cx[rows - HALO:, :]

    yield
    o_ref[:, 0:w] = _group_rms(lru_out, lg_ref[...], ones_bd).astype(o_ref.dtype)
    o_ref[:, w:2 * w] = _group_rms(sc_b * conv, scg_ref[...], ones_bd).astype(o_ref.dtype)


def _pair_bd(x):
    first = lax.broadcasted_iota(jnp.int32, x.shape, 1) < HEAD
    zero = jnp.zeros_like(x)
    return jnp.concatenate([jnp.where(first, x, zero), jnp.where(first, zero, x)], axis=0)


def _solve_unit_lower(lmats, rhss, same_blk, eye):
    dot = lambda a, b: jnp.dot(a, _pair_bd(b), preferred_element_type=F32)
    lds = [jnp.where(same_blk, l, 0.0) for l in lmats]
    los = [_bf(l - ld) for l, ld in zip(lmats, lds)]
    ts = [eye + ld for ld in lds]
    tbs = [_bf(t) for t in ts]
    pbs = [_bf(ld) for ld in lds]
    s = 2
    while s < INV_BLOCK:
        pbs = [_bf(dot(pb, pb)) for pb in pbs]
        ts = [t + dot(tb, pb) for t, tb, pb in zip(ts, tbs, pbs)]
        tbs = [_bf(t) for t in ts]
        s *= 2
    xs = [dot(tb, _bf(r)) for tb, r in zip(tbs, rhss)]
    pbs = [_bf(dot(tb, lo)) for tb, lo in zip(tbs, los)]
    nb = CHUNK // INV_BLOCK
    s = 1
    while s < nb:
        xs = [x + dot(pb, _bf(x)) for x, pb in zip(xs, pbs)]
        s *= 2
        if s < nb:
            pbs = [_bf(dot(pb, pb)) for pb in pbs]
    return xs


def _mixers_kernel(*refs, n_dl, n_il):
    (h_ref, gin_ref, win_ref), lru_prm, refs = refs[:3], refs[3:13], refs[13:]
    (mu_ref, w0_ref, a0_ref, wl_ref, kk_ref, ka_ref, rk_ref, lw_ref, lb_ref, bd_ref,
     ya_ref, o_ref, zl_ref, s_ref, p_ref, xh_ref, ch_ref, hs_ref) = refs
    width = w0_ref.shape[1]
    n_heads = width // HEAD
    n_slots = h_ref.shape[0]
    n_chunks = h_ref.shape[1] // CHUNK
    n_rows = n_slots * CHUNK
    ones_bd = bd_ref[...]

    @pl.when(pl.program_id(1) == 0)
    def _():
        for ref in (zl_ref, s_ref, xh_ref, ch_ref, hs_ref):
            ref[...] = jnp.zeros_like(ref)

    t_rows = h_ref.shape[1]
    n_rw = p_ref.shape[2]
    n_a = win_ref.shape[1] - n_rw
    xn = _bf(_rms(h_ref[...].reshape(n_slots * t_rows, h_ref.shape[2]), gin_ref[...]))
    p_a = jnp.dot(xn, win_ref[:, 0:n_a], preferred_element_type=F32)
    pending = [_lrusc_stages(p_a[s * t_rows:(s + 1) * t_rows], lru_prm, ones_bd, xh_ref.at[s],
                             ch_ref.at[s], hs_ref.at[s], ya_ref.at[s]) for s in range(n_slots)]
    piece = -(-n_rw // (PROJ_PIECES * LANES)) * LANES
    for c0 in range(0, n_rw, piece):
        c1 = min(c0 + piece, n_rw)
        p_rw = jnp.dot(xn, win_ref[:, n_a + c0:n_a + c1], preferred_element_type=F32)
        for s in range(n_slots):
            p_ref[s, :, c0:c1] = p_rw[s * t_rows:(s + 1) * t_rows]
        pending = [g for g in pending if next(g, True) is None]
    while pending:
        pending = [g for g in pending if next(g, True) is None]

    pair = 2 * HEAD
    n_pairs = width // pair
    c2 = 2 * CHUNK
    ri = lax.broadcasted_iota(jnp.int32, (c2, 2 * pair), 0)
    cm = jnp.bitwise_and(lax.broadcasted_iota(jnp.int32, (c2, 2 * pair), 1), CHUNK - 1)
    gmask = ((ri < CHUNK) & (cm < ri)) | ((ri >= CHUNK) & (cm <= ri - CHUNK))
    ti = lax.broadcasted_iota(jnp.int32, (CHUNK, pair), 0)
    si = jnp.bitwise_and(lax.broadcasted_iota(jnp.int32, (CHUNK, pair), 1), CHUNK - 1)
    shift = INV_BLOCK.bit_length() - 1
    same_blk = jnp.right_shift(ti, shift) == jnp.right_shift(si, shift)
    eye = (ti == si).astype(F32)
    ti = lax.broadcasted_iota(jnp.int32, (CHUNK, CHUNK), 0)
    tri = (lax.broadcasted_iota(jnp.int32, (CHUNK, CHUNK), 1) <= ti).astype(BF16)
    same_head = ((lax.broadcasted_iota(jnp.int32, (pair, pair), 0) < HEAD)
                 == (lax.broadcasted_iota(jnp.int32, (pair, pair), 1) < HEAD))
    bd_w = ones_bd.shape[0]

    def seg_sum(x):
        xb = _bf(x)
        parts = [jnp.dot(xb[:, c:c + bd_w], ones_bd, preferred_element_type=F32)
                 for c in range(0, x.shape[1], bd_w)]
        return jnp.concatenate(parts, axis=1)

    slots = range(n_slots)
    stack = lambda xs: jnp.concatenate(xs, axis=0)
    unstack = lambda x: [x[s * CHUNK:(s + 1) * CHUNK] for s in slots]

    def chunk_body(c, carry):
        r0 = pl.multiple_of(c * CHUNK, CHUNK)
        zs = [p_ref[s, pl.ds(r0, CHUNK), :] for s in slots]
        first_row = lax.broadcasted_iota(jnp.int32, (SUBLANES, zs[0].shape[1]), 0) < 1
        zz = []
        for s in slots:
            rolled = pltpu.roll(zs[s], 1, axis=0)
            zprev = jnp.concatenate(
                [jnp.where(first_row, zl_ref[s], rolled[0:SUBLANES]), rolled[SUBLANES:]], axis=0)
            zl_ref[s] = zs[s][CHUNK - 1:CHUNK, :]
            zz.append(zs[s] + (zprev - zs[s]) * mu_ref[...])
        zz = stack(zz)
        r = zz[:, 0:width]
        k = zz[:, width:2 * width]
        v = zz[:, 2 * width:3 * width]
        lo = zz[:, 3 * width:]
        lane = lax.broadcasted_iota(jnp.int32, lo.shape, 1)
        lo_act = jnp.where(lane < n_dl, jnp.tanh(lo),
                           jnp.where(lane < n_dl + n_il, lo, jax.nn.sigmoid(lo)))
        lora = _mm(lo_act, wl_ref[...])
        w_log = -_softplus(-(w0_ref[...] + lora[:, 0:width])) - 0.5
        lw = -jnp.exp(w_log)
        a = jax.nn.sigmoid(a0_ref[...] + lora[:, width:2 * width])
        g = lora[:, 2 * width:3 * width]
        kk = k * kk_ref[...]
        kf = k * (1.0 + (a - 1.0) * ka_ref[...])
        sums = seg_sum(jnp.concatenate([kk * kk, r * kf * rk_ref[...]], axis=0))
        kk = kk * lax.rsqrt(jnp.maximum(sums[0:n_rows], 1e-24))
        bonus = sums[n_rows:]
        kka = kk * a

        lw_hi = _bf(lw)
        lw_lo = _bf(lw - lw_hi.astype(F32))
        hi_l = jnp.concatenate(unstack(lw_hi), axis=1)
        lo_l = jnp.concatenate(unstack(lw_lo), axis=1)
        cum_l = (jnp.dot(tri, hi_l, preferred_element_type=F32)
                 + jnp.dot(tri, lo_l, preferred_element_type=F32))
        cum = stack([cum_l[:, s * width:(s + 1) * width] for s in slots])
        cl = stack([jnp.broadcast_to(cum_l[CHUNK - 1:CHUNK, s * width:(s + 1) * width],
                                     (CHUNK, width)) for s in slots])
        w_in = jnp.exp(-cum)
        w_rem = jnp.exp(cl - cum)
        al = unstack(_bf(-kk * jnp.exp(cum - lw)))
        be = unstack(_bf(kka * w_in))
        kt = unstack(_bf(kf * w_in))
        rt = unstack(_bf(r * jnp.exp(cum)))
        bw = unstack(_bf(kka * w_rem))
        kw = unstack(_bf(kf * w_rem))
        vb = unstack(_bf(v))
        w_c = jnp.exp(cum_l[CHUNK - 1:CHUNK, :])

        chains = [(s, slice(j * pair, (j + 1) * pair)) for s in slots for j in range(n_pairs)]
        cat = jnp.concatenate
        ars = [cat([al[s][:, ps], rt[s][:, ps]], axis=0) for s, ps in chains]
        bks = [cat([_pair_bd(be[s][:, ps]), _pair_bd(kt[s][:, ps])], axis=0) for s, ps in chains]
        bkws = [cat([bw[s][:, ps], kw[s][:, ps]], axis=0) for s, ps in chains]
        vps = [vb[s][:, ps] for s, ps in chains]
        vbds = [_pair_bd(v_p) for v_p in vps]
        wcs = [w_c[:, s * width + j * pair:s * width + (j + 1) * pair]
               for s in slots for j in range(n_pairs)]
        n_chains = n_slots * n_pairs
        s0s = [s_ref[i] for i in range(n_chains)]
        gms = [jnp.where(gmask, _mm_nt(ar, bk), 0.0) for ar, bk in zip(ars, bks)]
        a_ss = [_mm_nt(ar, s0) for ar, s0 in zip(ars, s0s)]
        akvs = [_mm(gm[0:CHUNK, pair:], vbd) for gm, vbd in zip(gms, vbds)]
        rhss = [a_s[0:CHUNK] + akv for a_s, akv in zip(a_ss, akvs)]
        es = _solve_unit_lower([gm[0:CHUNK, 0:pair] for gm in gms], rhss, same_blk, eye)
        ebs = [_bf(e) for e in es]
        ys = [a_s[CHUNK:] + _mm(gm[CHUNK:, :], cat([_pair_bd(eb), vbd], axis=0))
              for a_s, gm, eb, vbd in zip(a_ss, gms, ebs, vbds)]
        for i in range(n_chains):
            upd = _mm_tn(cat([ebs[i], vps[i]], axis=0), bkws[i])
            s_ref[i] = s0s[i] * wcs[i] + jnp.where(same_head, upd, 0.0)

        y = stack([cat(ys[s * n_pairs:(s + 1) * n_pairs], axis=1) for s in slots])
        mean = seg_sum(y) * (1.0 / HEAD)
        d = y - mean
        var = seg_sum(d * d) * (1.0 / HEAD)
        yn = d * lax.rsqrt(var + LNX_EPS) * lw_ref[...] + lb_ref[...]
        out = unstack(((yn + bonus * v) * g).astype(o_ref.dtype))
        for s in slots:
            o_ref[s, pl.ds(r0, CHUNK), :] = out[s]
        return carry

    lax.fori_loop(0, n_chunks, chunk_body, 0)


def _mixers(h, g_in, w_in, lru_params, rw_params, batch, ones_bd, n_dl, n_il):
    n, d = h.shape
    nrw = rw_params[0].shape[1]
    width = rw_params[1].shape[1]
    w_a = lru_params[1].shape[1]
    t_pad = n // batch
    blk = lambda b, t: (b, t, 0)
    params = (g_in, w_in) + tuple(lru_params) + tuple(rw_params) + (ones_bd,)
    out_blk = lambda w: pl.BlockSpec((RWKV_SLOTS, RWKV_ROWS, w), blk)
    y_a, y_rw = pl.pallas_call(
        functools.partial(_mixers_kernel, n_dl=n_dl, n_il=n_il),
        grid=(batch // RWKV_SLOTS, t_pad // RWKV_ROWS),
        in_specs=[pl.BlockSpec((RWKV_SLOTS, RWKV_ROWS, d), blk)] + [_resident(p) for p in params],
        out_specs=[out_blk(2 * w_a), out_blk(width)],
        out_shape=[jax.ShapeDtypeStruct((batch, t_pad, 2 * w_a), BF16),
                   jax.ShapeDtypeStruct((batch, t_pad, width), BF16)],
        scratch_shapes=[pltpu.VMEM((RWKV_SLOTS, 1, nrw), F32),
                        pltpu.VMEM((RWKV_SLOTS * (width // (2 * HEAD)), 2 * HEAD, 2 * HEAD), F32),
                        pltpu.VMEM((RWKV_SLOTS, RWKV_ROWS, nrw), F32),
                        pltpu.VMEM((RWKV_SLOTS, HALO, w_a), F32),
                        pltpu.VMEM((RWKV_SLOTS, HALO, w_a), F32),
                        pltpu.VMEM((RWKV_SLOTS, 1, w_a), F32)],
        compiler_params=_cparams(2),
        name="mixers",
    )(h.reshape(batch, t_pad, d), *[_operand(p) for p in params])
    return y_a.reshape(n, 2 * w_a), y_rw.reshape(n, width)


def _ones_block_diag(width):
    return jnp.asarray(np.kron(np.eye(width // HEAD), np.ones((HEAD, HEAD))), BF16)


def _block_diag(w):
    n, g, i, j = w.shape
    return jnp.einsum('lgij,gh->lgihj', w, jnp.eye(g, dtype=w.dtype)).reshape(n, g * i, g * j)


def kernel(x, meta_tokens, norm_g, ffn1_w_in, ffn1_w_out, ffn2_w_in, ffn2_w_out, mix_w_in, mix_w_out, lru_conv_w, lru_conv_b, lru_wa, lru_ba, lru_wx, lru_bx, lru_lambda, lru_norm_g, sc_conv_w, sc_norm_g, rwkv_mu, rwkv_w0, rwkv_w2, rwkv_a0, rwkv_a2, rwkv_g2, rwkv_k_k, rwkv_k_a, rwkv_r_k, rwkv_lnx_w, rwkv_lnx_b):
    bsz, seq, d = x.shape
    n_meta = meta_tokens.shape[0]
    depth = norm_g.shape[0]
    lru_w = lru_conv_b.shape[1]
    sc_w = sc_norm_g.shape[1]
    rw_w = rwkv_w0.shape[1]
    assert lru_w == sc_w
    t_real = n_meta + seq
    t_pad = -(-t_real // TIME_ALIGN) * TIME_ALIGN

    meta = jnp.broadcast_to(meta_tokens.astype(x.dtype)[None], (bsz, n_meta, d))
    pad = jnp.zeros((bsz, t_pad - t_real, d), x.dtype)
    h = jnp.concatenate([meta, x, pad], axis=1).reshape(bsz * t_pad, d)

    ones_bd = _ones_block_diag(lru_w)
    rows3 = lambda p: p.reshape(-1, 1, p.shape[-1])
    f1_in, f1_out, f2_in, f2_out = _bf(ffn1_w_in), _bf(ffn1_w_out), _bf(ffn2_w_in), _bf(ffn2_w_out)
    m_in, m_out = _bf(mix_w_in), _bf(mix_w_out)
    wa_bd, wx_bd = _bf(_block_diag(lru_wa)), _bf(_block_diag(lru_wx))
    n_dl, n_il, n_gl = rwkv_w2.shape[1], rwkv_a2.shape[1], rwkv_g2.shape[1]
    lora_w = jnp.zeros((depth, n_dl + n_il + n_gl, 3 * rw_w), F32)
    lora_w = lora_w.at[:, 0:n_dl, 0:rw_w].set(rwkv_w2)
    lora_w = lora_w.at[:, n_dl:n_dl + n_il, rw_w:2 * rw_w].set(rwkv_a2)
    lora_w = _bf(lora_w.at[:, n_dl + n_il:, 2 * rw_w:].set(rwkv_g2))
    norm_rows = rows3(norm_g)
    n_norms = norm_g.shape[1]
    lru_stacked = (lru_conv_w, rows3(lru_conv_b), wa_bd, rows3(lru_ba), wx_bd, rows3(lru_bx),
                   rows3(lru_lambda), rows3(lru_norm_g), sc_conv_w, rows3(sc_norm_g))
    rw_stacked = (rows3(rwkv_mu), rows3(rwkv_w0), rows3(rwkv_a0), lora_w, rows3(rwkv_k_k),
                  rows3(rwkv_k_a), rows3(rwkv_r_k), rows3(rwkv_lnx_w), rows3(rwkv_lnx_b))

    for l in range(depth):
        g = [_Layer(norm_rows, l * n_norms + i) for i in range(n_norms)]
        h = _ffn(h, g[0], _Layer(f1_in, l), _Layer(f1_out, l), g[1])
        y_a, y_rw = _mixers(h, g[2], _Layer(m_in, l), [_Layer(p, l) for p in lru_stacked],
                            [_Layer(p, l) for p in rw_stacked], bsz, ones_bd, n_dl, n_il)
        gather = None
        if l == depth - 1:
            per_seq = seq // OUT_ROWS
            align = math.gcd(t_pad, n_meta, OUT_ROWS)
            gather = (OUT_ROWS, bsz * per_seq, lambda i: pl.multiple_of(
                (i // per_seq) * t_pad + n_meta + (i % per_seq) * OUT_ROWS, align))
        h = _ffn(h, g[4], _Layer(f2_in, l), _Layer(f2_out, l), g[5],
                 mix=(y_a, y_rw, _Layer(m_out, l), g[3]), gather=gather)

    return h.reshape(bsz, seq, d)
```

```python
import functools
import math

import jax
import jax.numpy as jnp
import numpy as np
from jax import lax
from jax.experimental import pallas as pl
from jax.experimental.pallas import tpu as pltpu

F32 = jnp.float32
BF16 = jnp.bfloat16

RMS_EPS = 1e-6
LNX_EPS = 64e-5
LRU_C = 8.0

HEAD = 64
CHUNK = 64
INV_BLOCK = 16
RWKV_ROWS = 192
RWKV_SLOTS = 4
FFN_ROWS = 688
OUT_ROWS = 512
SUBLANES = 8
LANES = 128
PROJ_PIECES = 4
HALO = 8
VMEM_LIMIT = 56 * 1024 * 1024


def _cparams(n_axes):
    return pltpu.CompilerParams(
        dimension_semantics=("arbitrary",) * n_axes, vmem_limit_bytes=VMEM_LIMIT)


class _Layer:
    def __init__(self, stacked, layer):
        self.stacked, self.layer = stacked, layer
        self.shape = stacked.shape[1:]


def _operand(p):
    return p.stacked if isinstance(p, _Layer) else p


def _resident(p):
    zeros = (0,) * len(p.shape)
    if isinstance(p, _Layer):
        layer = p.layer
        return pl.BlockSpec((None,) + tuple(p.shape), lambda *_: (layer,) + zeros,
                            pipeline_mode=pl.Buffered(1))
    return pl.BlockSpec(p.shape, lambda *_: zeros, pipeline_mode=pl.Buffered(1))


def _rms(x, g):
    return x * lax.rsqrt(jnp.mean(x * x, axis=-1, keepdims=True) + RMS_EPS) * g


def _bf(x):
    return x.astype(BF16)


def _mm(a, b):
    return jnp.dot(_bf(a), _bf(b), preferred_element_type=F32)


def _mm_nt(a, b):
    return lax.dot_general(_bf(a), _bf(b), (((1,), (1,)), ((), ())), preferred_element_type=F32)


def _mm_tn(a, b):
    return lax.dot_general(_bf(a), _bf(b), (((0,), (0,)), ((), ())), preferred_element_type=F32)


def _softplus(x):
    return jnp.maximum(x, 0.0) + jnp.log(1.0 + jnp.exp(-jnp.abs(x)))


def _ffn_kernel(*refs, has_mix):
    if has_mix:
        h_ref, ya_ref, yrw_ref, wm_ref, gm_ref, gpre_ref, wi_ref, wo_ref, gpost_ref, o_ref = refs
    else:
        h_ref, gpre_ref, wi_ref, wo_ref, gpost_ref, o_ref = refs
    h = h_ref[...]
    if has_mix:
        na = ya_ref.shape[1]
        m = jnp.dot(ya_ref[...], wm_ref[:na, :], preferred_element_type=F32)
        m = m + jnp.dot(yrw_ref[...], wm_ref[na:, :], preferred_element_type=F32)
        h = h + _rms(m, gm_ref[...])
    d_ff = wo_ref.shape[0]
    xn = _bf(_rms(h, gpre_ref[...]))
    gu = jnp.dot(xn, wi_ref[...], preferred_element_type=F32)
    act = _bf(jax.nn.silu(gu[:, :d_ff]) * gu[:, d_ff:])
    y = jnp.dot(act, wo_ref[...], preferred_element_type=F32)
    o_ref[...] = h + 0.5 * _rms(y, gpost_ref[...])


def _ffn(h, g_pre, w_in, w_out, g_post, blocks, mix=None, dense_out=False):
    n, d = h.shape
    d_ff = w_out.shape[0]
    block_rows, n_blocks, first_row = blocks
    rows = lambda width: pl.BlockSpec((pl.Element(block_rows), pl.Element(width)),
                                      lambda i: (first_row(i), 0))
    operands, in_specs = [h], [rows(d)]
    params = (g_pre, w_in, w_out, g_post)
    if mix is not None:
        y_a, y_rw, w_mix, g_mix = mix
        operands += [y_a, y_rw]
        in_specs += [rows(y_a.shape[1]), rows(y_rw.shape[1])]
        params = (w_mix, g_mix) + params
    operands += [_operand(p) for p in params]
    in_specs += [_resident(p) for p in params]
    if dense_out:
        out_spec = pl.BlockSpec((block_rows, d), lambda i: (i, 0))
        out_shape, aliases = jax.ShapeDtypeStruct((n_blocks * block_rows, d), F32), {}
    else:
        out_spec, out_shape, aliases = rows(d), jax.ShapeDtypeStruct((n, d), F32), {0: 0}
    return pl.pallas_call(
        functools.partial(_ffn_kernel, has_mix=mix is not None),
        grid=(n_blocks,),
        in_specs=in_specs,
        out_specs=out_spec,
        out_shape=out_shape,
        input_output_aliases=aliases,
        compiler_params=_cparams(1),
        name="ffn_mix" if mix is not None else "ffn",
    )(*operands)


def _group_rms(x, g, ones_bd):
    ms = _mm(x * x, ones_bd) * (1.0 / HEAD)
    return x * lax.rsqrt(ms + RMS_EPS) * g


def _lrusc_stages(p, prm, ones_bd, xh_ref, ch_ref, hs_ref, o_ref):
    cw_ref, cb_ref, wa_ref, ba_ref, wx_ref, bx_ref, lam_ref, lg_ref, scw_ref, scg_ref = prm
    w = cb_ref.shape[1]
    rows = p.shape[0]
    xb = p[:, 0:w]
    gb = p[:, w:2 * w]
    sc_b = p[:, 2 * w:3 * w]
    cx = p[:, 3 * w:4 * w] * p[:, 4 * w:5 * w]

    xx = jnp.concatenate([xh_ref[...], xb], axis=0)
    u = xx[HALO - 3:HALO - 3 + rows] * cw_ref[0:1, :]
    for k in range(1, 4):
        u = u + xx[HALO - 3 + k:HALO - 3 + k + rows] * cw_ref[k:k + 1, :]
    u = u + cb_ref[...]
    xh_ref[...] = xb[rows - HALO:, :]

    yield
    r = jax.nn.sigmoid(_mm(u, wa_ref[...]) + ba_ref[...])
    gi = jax.nn.sigmoid(_mm(u, wx_ref[...]) + bx_ref[...])
    log_a = -LRU_C * r * _softplus(-lam_ref[...])
    a = jnp.exp(log_a)
    th = jnp.tanh(log_a)
    b = jnp.sqrt(-2.0 * th / (1.0 - th)) * (gi * u)

    sub = jnp.bitwise_and(lax.broadcasted_iota(jnp.int32, a.shape, 0), SUBLANES - 1)
    s = 1
    while s < SUBLANES:
        b = a * jnp.where(sub >= s, pltpu.roll(b, s, axis=0), 0.0) + b
        a = a * jnp.where(sub >= s, pltpu.roll(a, s, axis=0), 1.0)
        s *= 2
    carry = hs_ref[...]
    groups = []
    for r0 in range(0, rows, SUBLANES):
        hg = b[r0:r0 + SUBLANES] + a[r0:r0 + SUBLANES] * carry
        groups.append(hg)
        carry = hg[SUBLANES - 1:SUBLANES, :]
    hseq = jnp.concatenate(groups, axis=0)
    hs_ref[...] = carry
    lru_out = jax.nn.gelu(gb) * hseq

    cc = jnp.concatenate([ch_ref[...], cx], axis=0)
    conv = cc[HALO - 2:HALO - 2 + rows] * scw_ref[0:1, :]
    for k in range(1, 3):
        conv = conv + cc[HALO - 2 + k:HALO - 2 + k + rows] * scw_ref[k:k + 1, :]
    ch_ref[...] = cx[rows - HALO:, :]

    yield
    o_ref[:, 0:w] = _group_rms(lru_out, lg_ref[...], ones_bd).astype(o_ref.dtype)
    o_ref[:, w:2 * w] = _group_rms(sc_b * conv, scg_ref[...], ones_bd).astype(o_ref.dtype)


def _pair_bd(x):
    first = lax.broadcasted_iota(jnp.int32, x.shape, 1) < HEAD
    zero = jnp.zeros_like(x)
    return jnp.concatenate([jnp.where(first, x, zero), jnp.where(first, zero, x)], axis=0)


def _solve_unit_lower(lmats, rhss, same_blk, eye):
    dot = lambda a, b: jnp.dot(a, _pair_bd(b), preferred_element_type=F32)
    lds = [jnp.where(same_blk, l, 0.0) for l in lmats]
    los = [_bf(l - ld) for l, ld in zip(lmats, lds)]
    ts = [eye + ld for ld in lds]
    tbs = [_bf(t) for t in ts]
    pbs = [_bf(ld) for ld in lds]
    s = 2
    while s < INV_BLOCK:
        pbs = [_bf(dot(pb, pb)) for pb in pbs]
        ts = [t + dot(tb, pb) for t, tb, pb in zip(ts, tbs, pbs)]
        tbs = [_bf(t) for t in ts]
        s *= 2
    xs = [dot(tb, _bf(r)) for tb, r in zip(tbs, rhss)]
    pbs = [_bf(dot(tb, lo)) for tb, lo in zip(tbs, los)]
    nb = CHUNK // INV_BLOCK
    s = 1
    while s < nb:
        xs = [x + dot(pb, _bf(x)) for x, pb in zip(xs, pbs)]
        s *= 2
        if s < nb:
            pbs = [_bf(dot(pb, pb)) for pb in pbs]
    return xs


def _mixers_kernel(*refs, n_dl, n_il):
    (h_ref, gin_ref, win_ref), lru_prm, refs = refs[:3], refs[3:13], refs[13:]
    (mu_ref, w0_ref, a0_ref, wl_ref, kk_ref, ka_ref, rk_ref, lw_ref, lb_ref, bd_ref,
     ya_ref, o_ref, zl_ref, s_ref, p_ref, xh_ref, ch_ref, hs_ref) = refs
    width = w0_ref.shape[1]
    n_slots = h_ref.shape[0]
    n_chunks = h_ref.shape[1] // CHUNK
    n_rows = n_slots * CHUNK
    ones_bd = bd_ref[...]

    @pl.when(pl.program_id(1) == 0)
    def _():
        for ref in (zl_ref, s_ref, xh_ref, ch_ref, hs_ref):
            ref[...] = jnp.zeros_like(ref)

    t_rows = h_ref.shape[1]
    n_rw = p_ref.shape[2]
    n_a = win_ref.shape[1] - n_rw
    xn = _bf(_rms(h_ref[...].reshape(n_slots * t_rows, h_ref.shape[2]), gin_ref[...]))
    p_a = jnp.dot(xn, win_ref[:, 0:n_a], preferred_element_type=F32)
    pending = [_lrusc_stages(p_a[s * t_rows:(s + 1) * t_rows], lru_prm, ones_bd, xh_ref.at[s],
                             ch_ref.at[s], hs_ref.at[s], ya_ref.at[s]) for s in range(n_slots)]
    piece = -(-n_rw // (PROJ_PIECES * LANES)) * LANES
    for c0 in range(0, n_rw, piece):
        c1 = min(c0 + piece, n_rw)
        p_rw = jnp.dot(xn, win_ref[:, n_a + c0:n_a + c1], preferred_element_type=F32)
        for s in range(n_slots):
            p_ref[s, :, c0:c1] = p_rw[s * t_rows:(s + 1) * t_rows]
        pending = [g for g in pending if next(g, True) is None]
    while pending:
        pending = [g for g in pending if next(g, True) is None]

    pair = 2 * HEAD
    n_pairs = width // pair
    c2 = 2 * CHUNK
    ri = lax.broadcasted_iota(jnp.int32, (c2, 2 * pair), 0)
    cm = jnp.bitwise_and(lax.broadcasted_iota(jnp.int32, (c2, 2 * pair), 1), CHUNK - 1)
    gmask = ((ri < CHUNK) & (cm < ri)) | ((ri >= CHUNK) & (cm <= ri - CHUNK))
    ti = lax.broadcasted_iota(jnp.int32, (CHUNK, pair), 0)
    si = jnp.bitwise_and(lax.broadcasted_iota(jnp.int32, (CHUNK, pair), 1), CHUNK - 1)
    shift = INV_BLOCK.bit_length() - 1
    same_blk = jnp.right_shift(ti, shift) == jnp.right_shift(si, shift)
    eye = (ti == si).astype(F32)
    ti = lax.broadcasted_iota(jnp.int32, (CHUNK, CHUNK), 0)
    tri = (lax.broadcasted_iota(jnp.int32, (CHUNK, CHUNK), 1) <= ti).astype(BF16)
    same_head = ((lax.broadcasted_iota(jnp.int32, (pair, pair), 0) < HEAD)
                 == (lax.broadcasted_iota(jnp.int32, (pair, pair), 1) < HEAD))
    bd_w = ones_bd.shape[0]

    def seg_sum(x):
        xb = _bf(x)
        parts = [jnp.dot(xb[:, c:c + bd_w], ones_bd, preferred_element_type=F32)
                 for c in range(0, x.shape[1], bd_w)]
        return jnp.concatenate(parts, axis=1)

    slots = range(n_slots)
    stack = lambda xs: jnp.concatenate(xs, axis=0)
    unstack = lambda x: [x[s * CHUNK:(s + 1) * CHUNK] for s in slots]

    def chunk_body(c, carry):
        r0 = pl.multiple_of(c * CHUNK, CHUNK)
        zs = [p_ref[s, pl.ds(r0, CHUNK), :] for s in slots]
        first_row = lax.broadcasted_iota(jnp.int32, (SUBLANES, zs[0].shape[1]), 0) < 1
        zz = []
        for s in slots:
            rolled = pltpu.roll(zs[s], 1, axis=0)
            zprev = jnp.concatenate(
                [jnp.where(first_row, zl_ref[s], rolled[0:SUBLANES]), rolled[SUBLANES:]], axis=0)
            zl_ref[s] = zs[s][CHUNK - 1:CHUNK, :]
            zz.append(zs[s] + (zprev - zs[s]) * mu_ref[...])
        zz = stack(zz)
        r = zz[:, 0:width]
        k = zz[:, width:2 * width]
        v = zz[:, 2 * width:3 * width]
        lo = zz[:, 3 * width:]
        lane = lax.broadcasted_iota(jnp.int32, lo.shape, 1)
        lo_act = jnp.where(lane < n_dl, jnp.tanh(lo),
                           jnp.where(lane < n_dl + n_il, lo, jax.nn.sigmoid(lo)))
        lora = _mm(lo_act, wl_ref[...])
        w_log = -_softplus(-(w0_ref[...] + lora[:, 0:width])) - 0.5
        lw = -jnp.exp(w_log)
        a = jax.nn.sigmoid(a0_ref[...] + lora[:, width:2 * width])
        g = lora[:, 2 * width:3 * width]
        kk = k * kk_ref[...]
        kf = k * (1.0 + (a - 1.0) * ka_ref[...])
        sums = seg_sum(jnp.concatenate([kk * kk, r * kf * rk_ref[...]], axis=0))
        kk = kk * lax.rsqrt(jnp.maximum(sums[0:n_rows], 1e-24))
        bonus = sums[n_rows:]
        kka = kk * a

        lw_hi = _bf(lw)
        lw_lo = _bf(lw - lw_hi.astype(F32))
        hi_l = jnp.concatenate(unstack(lw_hi), axis=1)
        lo_l = jnp.concatenate(unstack(lw_lo), axis=1)
        cum_l = (jnp.dot(tri, hi_l, preferred_element_type=F32)
                 + jnp.dot(tri, lo_l, preferred_element_type=F32))
        cum = stack([cum_l[:, s * width:(s + 1) * width] for s in slots])
        cl = stack([jnp.broadcast_to(cum_l[CHUNK - 1:CHUNK, s * width:(s + 1) * width],
                                     (CHUNK, width)) for s in slots])
        w_in = jnp.exp(-cum)
        w_rem = jnp.exp(cl - cum)
        al = unstack(_bf(-kk * jnp.exp(cum - lw)))
        be = unstack(_bf(kka * w_in))
        kt = unstack(_bf(kf * w_in))
        rt = unstack(_bf(r * jnp.exp(cum)))
        bw = unstack(_bf(kka * w_rem))
        kw = unstack(_bf(kf * w_rem))
        vb = unstack(_bf(v))
        w_c = jnp.exp(cum_l[CHUNK - 1:CHUNK, :])

        chains = [(s, slice(j * pair, (j + 1) * pair)) for s in slots for j in range(n_pairs)]
        cat = jnp.concatenate
        ars = [cat([al[s][:, ps], rt[s][:, ps]], axis=0) for s, ps in chains]
        bks = [cat([_pair_bd(be[s][:, ps]), _pair_bd(kt[s][:, ps])], axis=0) for s, ps in chains]
        bkws = [cat([bw[s][:, ps], kw[s][:, ps]], axis=0) for s, ps in chains]
        vps = [vb[s][:, ps] for s, ps in chains]
        vbds = [_pair_bd(v_p) for v_p in vps]
        wcs = [w_c[:, s * width + j * pair:s * width + (j + 1) * pair]
               for s in slots for j in range(n_pairs)]
        n_chains = n_slots * n_pairs
        s0s = [s_ref[i] for i in range(n_chains)]
        gms = [jnp.where(gmask, _mm_nt(ar, bk), 0.0) for ar, bk in zip(ars, bks)]
        a_ss = [_mm_nt(ar, s0) for ar, s0 in zip(ars, s0s)]
        akvs = [_mm(gm[0:CHUNK, pair:], vbd) for gm, vbd in zip(gms, vbds)]
        rhss = [a_s[0:CHUNK] + akv for a_s, akv in zip(a_ss, akvs)]
        es = _solve_unit_lower([gm[0:CHUNK, 0:pair] for gm in gms], rhss, same_blk, eye)
        ebs = [_bf(e) for e in es]
        ys = [a_s[CHUNK:] + _mm(gm[CHUNK:, :], cat([_pair_bd(eb), vbd], axis=0))
              for a_s, gm, eb, vbd in zip(a_ss, gms, ebs, vbds)]
        for i in range(n_chains):
            upd = _mm_tn(cat([ebs[i], vps[i]], axis=0), bkws[i])
            s_ref[i] = s0s[i] * wcs[i] + jnp.where(same_head, upd, 0.0)

        y = stack([cat(ys[s * n_pairs:(s + 1) * n_pairs], axis=1) for s in slots])
        mean = seg_sum(y) * (1.0 / HEAD)
        d = y - mean
        var = seg_sum(d * d) * (1.0 / HEAD)
        yn = d * lax.rsqrt(var + LNX_EPS) * lw_ref[...] + lb_ref[...]
        out = unstack(((yn + bonus * v) * g).astype(o_ref.dtype))
        for s in slots:
            o_ref[s, pl.ds(r0, CHUNK), :] = out[s]
        return carry

    lax.fori_loop(0, n_chunks, chunk_body, 0)


def _mixers(h, g_in, w_in, lru_params, rw_params, batch, ones_bd, n_dl, n_il):
    n, d = h.shape
    nrw = rw_params[0].shape[1]
    width = rw_params[1].shape[1]
    w_a = lru_params[1].shape[1]
    t_pad = n // batch
    blk = lambda b, t: (b, t, 0)
    params = (g_in, w_in) + tuple(lru_params) + tuple(rw_params) + (ones_bd,)
    out_blk = lambda w: pl.BlockSpec((RWKV_SLOTS, RWKV_ROWS, w), blk)
    y_a, y_rw = pl.pallas_call(
        functools.partial(_mixers_kernel, n_dl=n_dl, n_il=n_il),
        grid=(batch // RWKV_SLOTS, t_pad // RWKV_ROWS),
        in_specs=[pl.BlockSpec((RWKV_SLOTS, RWKV_ROWS, d), blk)] + [_resident(p) for p in params],
        out_specs=[out_blk(2 * w_a), out_blk(width)],
        out_shape=[jax.ShapeDtypeStruct((batch, t_pad, 2 * w_a), BF16),
                   jax.ShapeDtypeStruct((batch, t_pad, width), BF16)],
        scratch_shapes=[pltpu.VMEM((RWKV_SLOTS, 1, nrw), F32),
                        pltpu.VMEM((RWKV_SLOTS * (width // (2 * HEAD)), 2 * HEAD, 2 * HEAD), F32),
                        pltpu.VMEM((RWKV_SLOTS, RWKV_ROWS, nrw), F32),
                        pltpu.VMEM((RWKV_SLOTS, HALO, w_a), F32),
                        pltpu.VMEM((RWKV_SLOTS, HALO, w_a), F32),
                        pltpu.VMEM((RWKV_SLOTS, 1, w_a), F32)],
        compiler_params=_cparams(2),
        name="mixers",
    )(h.reshape(batch, t_pad, d), *[_operand(p) for p in params])
    return y_a.reshape(n, 2 * w_a), y_rw.reshape(n, width)


def _ones_block_diag(width):
    return jnp.asarray(np.kron(np.eye(width // HEAD), np.ones((HEAD, HEAD))), BF16)


def _block_diag(w):
    n, g, i, j = w.shape
    return jnp.einsum('lgij,gh->lgihj', w, jnp.eye(g, dtype=w.dtype)).reshape(n, g * i, g * j)


def kernel(x, meta_tokens, norm_g, ffn1_w_in, ffn1_w_out, ffn2_w_in, ffn2_w_out, mix_w_in, mix_w_out, lru_conv_w, lru_conv_b, lru_wa, lru_ba, lru_wx, lru_bx, lru_lambda, lru_norm_g, sc_conv_w, sc_norm_g, rwkv_mu, rwkv_w0, rwkv_w2, rwkv_a0, rwkv_a2, rwkv_g2, rwkv_k_k, rwkv_k_a, rwkv_r_k, rwkv_lnx_w, rwkv_lnx_b):
    bsz, seq, d = x.shape
    n_meta = meta_tokens.shape[0]
    depth = norm_g.shape[0]
    lru_w = lru_conv_b.shape[1]
    sc_w = sc_norm_g.shape[1]
    rw_w = rwkv_w0.shape[1]
    t_real = n_meta + seq
    t_pad = -(-t_real // RWKV_ROWS) * RWKV_ROWS
    assert lru_w == sc_w and bsz % RWKV_SLOTS == 0
    assert t_real % FFN_ROWS == 0 and seq % OUT_ROWS == 0

    meta = jnp.broadcast_to(meta_tokens.astype(x.dtype)[None], (bsz, n_meta, d))
    pad = jnp.zeros((bsz, t_pad - t_real, d), x.dtype)
    h = jnp.concatenate([meta, x, pad], axis=1).reshape(bsz * t_pad, d)

    ones_bd = _ones_block_diag(lru_w)
    rows3 = lambda p: p.reshape(-1, 1, p.shape[-1])
    f1_in, f1_out, f2_in, f2_out = _bf(ffn1_w_in), _bf(ffn1_w_out), _bf(ffn2_w_in), _bf(ffn2_w_out)
    m_in, m_out = _bf(mix_w_in), _bf(mix_w_out)
    wa_bd, wx_bd = _bf(_block_diag(lru_wa)), _bf(_block_diag(lru_wx))
    n_dl, n_il, n_gl = rwkv_w2.shape[1], rwkv_a2.shape[1], rwkv_g2.shape[1]
    lora_w = jnp.zeros((depth, n_dl + n_il + n_gl, 3 * rw_w), F32)
    lora_w = lora_w.at[:, 0:n_dl, 0:rw_w].set(rwkv_w2)
    lora_w = lora_w.at[:, n_dl:n_dl + n_il, rw_w:2 * rw_w].set(rwkv_a2)
    lora_w = _bf(lora_w.at[:, n_dl + n_il:, 2 * rw_w:].set(rwkv_g2))
    norm_rows = rows3(norm_g)
    n_norms = norm_g.shape[1]
    lru_stacked = (lru_conv_w, rows3(lru_conv_b), wa_bd, rows3(lru_ba), wx_bd, rows3(lru_bx),
                   rows3(lru_lambda), rows3(lru_norm_g), sc_conv_w, rows3(sc_norm_g))
    rw_stacked = (rows3(rwkv_mu), rows3(rwkv_w0), rows3(rwkv_a0), lora_w, rows3(rwkv_k_k),
                  rows3(rwkv_k_a), rows3(rwkv_r_k), rows3(rwkv_lnx_w), rows3(rwkv_lnx_b))

    def row_blocks(block_rows, first, count):
        align = math.gcd(t_pad, first, block_rows)
        per_seq = count // block_rows
        return (block_rows, bsz * per_seq, lambda i: pl.multiple_of(
            (i // per_seq) * t_pad + first + (i % per_seq) * block_rows, align))

    real_rows = row_blocks(FFN_ROWS, 0, t_real)
    out_rows = row_blocks(OUT_ROWS, n_meta, seq)

    for l in range(depth):
        g = [_Layer(norm_rows, l * n_norms + i) for i in range(n_norms)]
        h = _ffn(h, g[0], _Layer(f1_in, l), _Layer(f1_out, l), g[1], real_rows)
        y_a, y_rw = _mixers(h, g[2], _Layer(m_in, l), [_Layer(p, l) for p in lru_stacked],
                            [_Layer(p, l) for p in rw_stacked], bsz, ones_bd, n_dl, n_il)
        last = l == depth - 1
        h = _ffn(h, g[4], _Layer(f2_in, l), _Layer(f2_out, l), g[5],
                 out_rows if last else real_rows,
                 mix=(y_a, y_rw, _Layer(m_out, l), g[3]), dense_out=last)

    return h.reshape(bsz, seq, d)
```

```python
import functools
import math

import jax
import jax.numpy as jnp
import numpy as np
from jax import lax
from jax.experimental import pallas as pl
from jax.experimental.pallas import tpu as pltpu

F32 = jnp.float32
BF16 = jnp.bfloat16

RMS_EPS = 1e-6
LNX_EPS = 64e-5
LRU_C = 8.0

HEAD = 64
CHUNK = 64
INV_BLOCK = 16
RWKV_ROWS = 192
RWKV_SLOTS = 4
FFN_ROWS = 688
OUT_ROWS = 512
SUBLANES = 8
LANES = 128
PROJ_PIECES = 4
HALO = 8
VMEM_LIMIT = 56 * 1024 * 1024


def _cparams(n_axes):
    return pltpu.CompilerParams(
        dimension_semantics=("arbitrary",) * n_axes, vmem_limit_bytes=VMEM_LIMIT)


class _Layer:
    def __init__(self, stacked, layer):
        self.stacked, self.layer = stacked, layer
        self.shape = stacked.shape[1:]


def _operand(p):
    return p.stacked if isinstance(p, _Layer) else p


def _resident(p):
    zeros = (0,) * len(p.shape)
    if isinstance(p, _Layer):
        layer = p.layer
        return pl.BlockSpec((None,) + tuple(p.shape), lambda *_: (layer,) + zeros,
                            pipeline_mode=pl.Buffered(1))
    return pl.BlockSpec(p.shape, lambda *_: zeros, pipeline_mode=pl.Buffered(1))


def _rms(x, g):
    return x * lax.rsqrt(jnp.mean(x * x, axis=-1, keepdims=True) + RMS_EPS) * g


def _bf(x):
    return x.astype(BF16)


def _mm(a, b):
    return jnp.dot(_bf(a), _bf(b), preferred_element_type=F32)


def _mm_nt(a, b):
    return lax.dot_general(_bf(a), _bf(b), (((1,), (1,)), ((), ())), preferred_element_type=F32)


def _mm_tn(a, b):
    return lax.dot_general(_bf(a), _bf(b), (((0,), (0,)), ((), ())), preferred_element_type=F32)


def _softplus(x):
    return jnp.maximum(x, 0.0) + jnp.log(1.0 + jnp.exp(-jnp.abs(x)))


def _ffn_kernel(*refs, has_mix, embed_blocks):
    if has_mix:
        h_ref, ya_ref, yrw_ref, wm_ref, gm_ref, gpre_ref, wi_ref, wo_ref, gpost_ref, o_ref = refs
    elif embed_blocks:
        _, h_ref, meta_ref, gpre_ref, wi_ref, wo_ref, gpost_ref, o_ref = refs
    else:
        h_ref, gpre_ref, wi_ref, wo_ref, gpost_ref, o_ref = refs
    h = h_ref[...]
    if embed_blocks:
        n_meta = meta_ref.shape[0]
        shifted = jnp.concatenate([meta_ref[...], h[0:h.shape[0] - n_meta]], axis=0)
        h = jnp.where(lax.rem(pl.program_id(0), embed_blocks) == 0, shifted, h)
    if has_mix:
        na = ya_ref.shape[1]
        m = jnp.dot(ya_ref[...], wm_ref[:na, :], preferred_element_type=F32)
        m = m + jnp.dot(yrw_ref[...], wm_ref[na:, :], preferred_element_type=F32)
        h = h + _rms(m, gm_ref[...])
    d_ff = wo_ref.shape[0]
    xn = _bf(_rms(h, gpre_ref[...]))
    gu = jnp.dot(xn, wi_ref[...], preferred_element_type=F32)
    act = _bf(jax.nn.silu(gu[:, :d_ff]) * gu[:, d_ff:])
    y = jnp.dot(act, wo_ref[...], preferred_element_type=F32)
    o_ref[...] = h + 0.5 * _rms(y, gpost_ref[...])


def _ffn(h, g_pre, w_in, w_out, g_post, blocks, mix=None, dense_out=False, embed=None):
    n, d = h.shape
    d_ff = w_out.shape[0]
    block_rows, n_blocks, first_row = blocks
    element_rows = lambda width, start: pl.BlockSpec(
        (pl.Element(block_rows), pl.Element(width)), lambda i: (start(i), 0))
    rows = lambda width: element_rows(width, first_row)
    operands, in_specs = [h], [rows(d)]
    params = (g_pre, w_in, w_out, g_post)
    if embed is not None:
        x, meta, _, x_first_row = embed
        operands += [x, meta]
        in_specs = [pl.BlockSpec(memory_space=pl.ANY), element_rows(d, x_first_row),
                    _resident(meta)]
    if mix is not None:
        y_a, y_rw, w_mix, g_mix = mix
        operands += [y_a, y_rw]
        in_specs += [rows(y_a.shape[1]), rows(y_rw.shape[1])]
        params = (w_mix, g_mix) + params
    operands += [_operand(p) for p in params]
    in_specs += [_resident(p) for p in params]
    if dense_out:
        out_spec = pl.BlockSpec((block_rows, d), lambda i: (i, 0))
        out_shape, aliases = jax.ShapeDtypeStruct((n_blocks * block_rows, d), F32), {}
    else:
        out_spec, out_shape, aliases = rows(d), jax.ShapeDtypeStruct((n, d), F32), {0: 0}
    return pl.pallas_call(
        functools.partial(_ffn_kernel, has_mix=mix is not None,
                          embed_blocks=embed[2] if embed is not None else 0),
        grid=(n_blocks,),
        in_specs=in_specs,
        out_specs=out_spec,
        out_shape=out_shape,
        input_output_aliases=aliases,
        compiler_params=_cparams(1),
        name="ffn_mix" if mix is not None else "ffn",
    )(*operands)


def _group_rms(x, g, ones_bd):
    ms = _mm(x * x, ones_bd) * (1.0 / HEAD)
    return x * lax.rsqrt(ms + RMS_EPS) * g


def _lrusc_stages(p, prm, ones_bd, xh_ref, ch_ref, hs_ref, o_ref):
    cw_ref, cb_ref, wa_ref, ba_ref, wx_ref, bx_ref, lam_ref, lg_ref, scw_ref, scg_ref = prm
    w = cb_ref.shape[1]
    rows = p.shape[0]
    xb = p[:, 0:w]
    gb = p[:, w:2 * w]
    sc_b = p[:, 2 * w:3 * w]
    cx = p[:, 3 * w:4 * w] * p[:, 4 * w:5 * w]

    xx = jnp.concatenate([xh_ref[...], xb], axis=0)
    u = xx[HALO - 3:HALO - 3 + rows] * cw_ref[0:1, :]
    for k in range(1, 4):
        u = u + xx[HALO - 3 + k:HALO - 3 + k + rows] * cw_ref[k:k + 1, :]
    u = u + cb_ref[...]
    xh_ref[...] = xb[rows - HALO:, :]

    yield
    r = jax.nn.sigmoid(_mm(u, wa_ref[...]) + ba_ref[...])
    gi = jax.nn.sigmoid(_mm(u, wx_ref[...]) + bx_ref[...])
    log_a = -LRU_C * r * _softplus(-lam_ref[...])
    a = jnp.exp(log_a)
    th = jnp.tanh(log_a)
    b = jnp.sqrt(-2.0 * th / (1.0 - th)) * (gi * u)

    sub = jnp.bitwise_and(lax.broadcasted_iota(jnp.int32, a.shape, 0), SUBLANES - 1)
    s = 1
    while s < SUBLANES:
        b = a * jnp.where(sub >= s, pltpu.roll(b, s, axis=0), 0.0) + b
        a = a * jnp.where(sub >= s, pltpu.roll(a, s, axis=0), 1.0)
        s *= 2
    carry = hs_ref[...]
    groups = []
    for r0 in range(0, rows, SUBLANES):
        hg = b[r0:r0 + SUBLANES] + a[r0:r0 + SUBLANES] * carry
        groups.append(hg)
        carry = hg[SUBLANES - 1:SUBLANES, :]
    hseq = jnp.concatenate(groups, axis=0)
    hs_ref[...] = carry
    lru_out = jax.nn.gelu(gb) * hseq

    cc = jnp.concatenate([ch_ref[...], cx], axis=0)
    conv = cc[HALO - 2:HALO - 2 + rows] * scw_ref[0:1, :]
    for k in range(1, 3):
        conv = conv + cc[HALO - 2 + k:HALO - 2 + k + rows] * scw_ref[k:k + 1, :]
    ch_ref[...] = cx[rows - HALO:, :]

    yield
    o_ref[:, 0:w] = _group_rms(lru_out, lg_ref[...], ones_bd).astype(o_ref.dtype)
    o_ref[:, w:2 * w] = _group_rms(sc_b * conv, scg_ref[...], ones_bd).astype(o_ref.dtype)


def _pair_bd(x):
    first = lax.broadcasted_iota(jnp.int32, x.shape, 1) < HEAD
    zero = jnp.zeros_like(x)
    return jnp.concatenate([jnp.where(first, x, zero), jnp.where(first, zero, x)], axis=0)


def _solve_unit_lower(lmats, rhss, same_blk, eye):
    dot = lambda a, b: jnp.dot(a, _pair_bd(b), preferred_element_type=F32)
    lds = [jnp.where(same_blk, l, 0.0) for l in lmats]
    los = [_bf(l - ld) for l, ld in zip(lmats, lds)]
    ts = [eye + ld for ld in lds]
    tbs = [_bf(t) for t in ts]
    pbs = [_bf(ld) for ld in lds]
    s = 2
    while s < INV_BLOCK:
        pbs = [_bf(dot(pb, pb)) for pb in pbs]
        ts = [t + dot(tb, pb) for t, tb, pb in zip(ts, tbs, pbs)]
        tbs = [_bf(t) for t in ts]
        s *= 2
    xs = [dot(tb, _bf(r)) for tb, r in zip(tbs, rhss)]
    pbs = [_bf(dot(tb, lo)) for tb, lo in zip(tbs, los)]
    nb = CHUNK // INV_BLOCK
    s = 1
    while s < nb:
        xs = [x + dot(pb, _bf(x)) for x, pb in zip(xs, pbs)]
        s *= 2
        if s < nb:
            pbs = [_bf(dot(pb, pb)) for pb in pbs]
    return xs


def _mixers_kernel(*refs, n_dl, n_il):
    (h_ref, gin_ref, win_ref), lru_prm, refs = refs[:3], refs[3:13], refs[13:]
    (mu_ref, w0_ref, a0_ref, wl_ref, kk_ref, ka_ref, rk_ref, lw_ref, lb_ref, bd_ref,
     ya_ref, o_ref, zl_ref, s_ref, p_ref, xh_ref, ch_ref, hs_ref) = refs
    width = w0_ref.shape[1]
    n_slots = h_ref.shape[0]
    n_chunks = h_ref.shape[1] // CHUNK
    n_rows = n_slots * CHUNK
    ones_bd = bd_ref[...]

    @pl.when(pl.program_id(1) == 0)
    def _():
        for ref in (zl_ref, s_ref, xh_ref, ch_ref, hs_ref):
            ref[...] = jnp.zeros_like(ref)

    t_rows = h_ref.shape[1]
    n_rw = p_ref.shape[2]
    n_a = win_ref.shape[1] - n_rw
    xn = _bf(_rms(h_ref[...].reshape(n_slots * t_rows, h_ref.shape[2]), gin_ref[...]))
    p_a = jnp.dot(xn, win_ref[:, 0:n_a], preferred_element_type=F32)
    pending = [_lrusc_stages(p_a[s * t_rows:(s + 1) * t_rows], lru_prm, ones_bd, xh_ref.at[s],
                             ch_ref.at[s], hs_ref.at[s], ya_ref.at[s]) for s in range(n_slots)]
    piece = -(-n_rw // (PROJ_PIECES * LANES)) * LANES
    for c0 in range(0, n_rw, piece):
        c1 = min(c0 + piece, n_rw)
        p_rw = jnp.dot(xn, win_ref[:, n_a + c0:n_a + c1], preferred_element_type=F32)
        for s in range(n_slots):
            p_ref[s, :, c0:c1] = p_rw[s * t_rows:(s + 1) * t_rows]
        pending = [g for g in pending if next(g, True) is None]
    while pending:
        pending = [g for g in pending if next(g, True) is None]

    pair = 2 * HEAD
    n_pairs = width // pair
    c2 = 2 * CHUNK
    ri = lax.broadcasted_iota(jnp.int32, (c2, 2 * pair), 0)
    cm = jnp.bitwise_and(lax.broadcasted_iota(jnp.int32, (c2, 2 * pair), 1), CHUNK - 1)
    gmask = ((ri < CHUNK) & (cm < ri)) | ((ri >= CHUNK) & (cm <= ri - CHUNK))
    ti = lax.broadcasted_iota(jnp.int32, (CHUNK, pair), 0)
    si = jnp.bitwise_and(lax.broadcasted_iota(jnp.int32, (CHUNK, pair), 1), CHUNK - 1)
    shift = INV_BLOCK.bit_length() - 1
    same_blk = jnp.right_shift(ti, shift) == jnp.right_shift(si, shift)
    eye = (ti == si).astype(F32)
    ti = lax.broadcasted_iota(jnp.int32, (CHUNK, CHUNK), 0)
    tri = (lax.broadcasted_iota(jnp.int32, (CHUNK, CHUNK), 1) <= ti).astype(BF16)
    same_head = ((lax.broadcasted_iota(jnp.int32, (pair, pair), 0) < HEAD)
                 == (lax.broadcasted_iota(jnp.int32, (pair, pair), 1) < HEAD))
    bd_w = ones_bd.shape[0]

    def seg_sum(x):
        xb = _bf(x)
        parts = [jnp.dot(xb[:, c:c + bd_w], ones_bd, preferred_element_type=F32)
                 for c in range(0, x.shape[1], bd_w)]
        return jnp.concatenate(parts, axis=1)

    slots = range(n_slots)
    stack = lambda xs: jnp.concatenate(xs, axis=0)
    unstack = lambda x: [x[s * CHUNK:(s + 1) * CHUNK] for s in slots]

    def chunk_body(c, carry):
        r0 = pl.multiple_of(c * CHUNK, CHUNK)
        zs = [p_ref[s, pl.ds(r0, CHUNK), :] for s in slots]
        first_row = lax.broadcasted_iota(jnp.int32, (SUBLANES, zs[0].shape[1]), 0) < 1
        zz = []
        for s in slots:
            rolled = pltpu.roll(zs[s], 1, axis=0)
            zprev = jnp.concatenate(
                [jnp.where(first_row, zl_ref[s], rolled[0:SUBLANES]), rolled[SUBLANES:]], axis=0)
            zl_ref[s] = zs[s][CHUNK - 1:CHUNK, :]
            zz.append(zs[s] + (zprev - zs[s]) * mu_ref[...])
        zz = stack(zz)
        r = zz[:, 0:width]
        k = zz[:, width:2 * width]
        v = zz[:, 2 * width:3 * width]
        lo = zz[:, 3 * width:]
        lane = lax.broadcasted_iota(jnp.int32, lo.shape, 1)
        lo_act = jnp.where(lane < n_dl, jnp.tanh(lo),
                           jnp.where(lane < n_dl + n_il, lo, jax.nn.sigmoid(lo)))
        lora = _mm(lo_act, wl_ref[...])
        w_log = -_softplus(-(w0_ref[...] + lora[:, 0:width])) - 0.5
        lw = -jnp.exp(w_log)
        a = jax.nn.sigmoid(a0_ref[...] + lora[:, width:2 * width])
        g = lora[:, 2 * width:3 * width]
        kk = k * kk_ref[...]
        kf = k * (1.0 + (a - 1.0) * ka_ref[...])
        sums = seg_sum(jnp.concatenate([kk * kk, r * kf * rk_ref[...]], axis=0))
        kk = kk * lax.rsqrt(jnp.maximum(sums[0:n_rows], 1e-24))
        bonus = sums[n_rows:]
        kka = kk * a

        lw_hi = _bf(lw)
        lw_lo = _bf(lw - lw_hi.astype(F32))
        hi_l = jnp.concatenate(unstack(lw_hi), axis=1)
        lo_l = jnp.concatenate(unstack(lw_lo), axis=1)
        cum_l = (jnp.dot(tri, hi_l, preferred_element_type=F32)
                 + jnp.dot(tri, lo_l, preferred_element_type=F32))
        cum = stack([cum_l[:, s * width:(s + 1) * width] for s in slots])
        cl = stack([jnp.broadcast_to(cum_l[CHUNK - 1:CHUNK, s * width:(s + 1) * width],
                                     (CHUNK, width)) for s in slots])
        w_in = jnp.exp(-cum)
        w_rem = jnp.exp(cl - cum)
        al = unstack(_bf(-kk * jnp.exp(cum - lw)))
        be = unstack(_bf(kka * w_in))
        kt = unstack(_bf(kf * w_in))
        rt = unstack(_bf(r * jnp.exp(cum)))
        bw = unstack(_bf(kka * w_rem))
        kw = unstack(_bf(kf * w_rem))
        vb = unstack(_bf(v))
        w_c = jnp.exp(cum_l[CHUNK - 1:CHUNK, :])

        chains = [(s, slice(j * pair, (j + 1) * pair)) for s in slots for j in range(n_pairs)]
        cat = jnp.concatenate
        ars = [cat([al[s][:, ps], rt[s][:, ps]], axis=0) for s, ps in chains]
        bks = [cat([_pair_bd(be[s][:, ps]), _pair_bd(kt[s][:, ps])], axis=0) for s, ps in chains]
        bkws = [cat([bw[s][:, ps], kw[s][:, ps]], axis=0) for s, ps in chains]
        vps = [vb[s][:, ps] for s, ps in chains]
        vbds = [_pair_bd(v_p) for v_p in vps]
        wcs = [w_c[:, s * width + j * pair:s * width + (j + 1) * pair]
               for s in slots for j in range(n_pairs)]
        n_chains = n_slots * n_pairs
        s0s = [s_ref[i] for i in range(n_chains)]
        gms = [jnp.where(gmask, _mm_nt(ar, bk), 0.0) for ar, bk in zip(ars, bks)]
        a_ss = [_mm_nt(ar, s0) for ar, s0 in zip(ars, s0s)]
        akvs = [_mm(gm[0:CHUNK, pair:], vbd) for gm, vbd in zip(gms, vbds)]
        rhss = [a_s[0:CHUNK] + akv for a_s, akv in zip(a_ss, akvs)]
        es = _solve_unit_lower([gm[0:CHUNK, 0:pair] for gm in gms], rhss, same_blk, eye)
        ebs = [_bf(e) for e in es]
        ys = [a_s[CHUNK:] + _mm(gm[CHUNK:, :], cat([_pair_bd(eb), vbd], axis=0))
              for a_s, gm, eb, vbd in zip(a_ss, gms, ebs, vbds)]
        for i in range(n_chains):
            upd = _mm_tn(cat([ebs[i], vps[i]], axis=0), bkws[i])
            s_ref[i] = s0s[i] * wcs[i] + jnp.where(same_head, upd, 0.0)

        y = stack([cat(ys[s * n_pairs:(s + 1) * n_pairs], axis=1) for s in slots])
        mean = seg_sum(y) * (1.0 / HEAD)
        d = y - mean
        var = seg_sum(d * d) * (1.0 / HEAD)
        yn = d * lax.rsqrt(var + LNX_EPS) * lw_ref[...] + lb_ref[...]
        out = unstack(((yn + bonus * v) * g).astype(o_ref.dtype))
        for s in slots:
            o_ref[s, pl.ds(r0, CHUNK), :] = out[s]
        return carry

    lax.fori_loop(0, n_chunks, chunk_body, 0)


def _mixers(h, g_in, w_in, lru_params, rw_params, batch, ones_bd, n_dl, n_il):
    n, d = h.shape
    nrw = rw_params[0].shape[1]
    width = rw_params[1].shape[1]
    w_a = lru_params[1].shape[1]
    t_pad = n // batch
    blk = lambda b, t: (b, t, 0)
    params = (g_in, w_in) + tuple(lru_params) + tuple(rw_params) + (ones_bd,)
    out_blk = lambda w: pl.BlockSpec((RWKV_SLOTS, RWKV_ROWS, w), blk)
    y_a, y_rw = pl.pallas_call(
        functools.partial(_mixers_kernel, n_dl=n_dl, n_il=n_il),
        grid=(batch // RWKV_SLOTS, t_pad // RWKV_ROWS),
        in_specs=[pl.BlockSpec((RWKV_SLOTS, RWKV_ROWS, d), blk)] + [_resident(p) for p in params],
        out_specs=[out_blk(2 * w_a), out_blk(width)],
        out_shape=[jax.ShapeDtypeStruct((batch, t_pad, 2 * w_a), BF16),
                   jax.ShapeDtypeStruct((batch, t_pad, width), BF16)],
        scratch_shapes=[pltpu.VMEM((RWKV_SLOTS, 1, nrw), F32),
                        pltpu.VMEM((RWKV_SLOTS * (width // (2 * HEAD)), 2 * HEAD, 2 * HEAD), F32),
                        pltpu.VMEM((RWKV_SLOTS, RWKV_ROWS, nrw), F32),
                        pltpu.VMEM((RWKV_SLOTS, HALO, w_a), F32),
                        pltpu.VMEM((RWKV_SLOTS, HALO, w_a), F32),
                        pltpu.VMEM((RWKV_SLOTS, 1, w_a), F32)],
        compiler_params=_cparams(2),
        name="mixers",
    )(h.reshape(batch, t_pad, d), *[_operand(p) for p in params])
    return y_a.reshape(n, 2 * w_a), y_rw.reshape(n, width)


def _ones_block_diag(width):
    return jnp.asarray(np.kron(np.eye(width // HEAD), np.ones((HEAD, HEAD))), BF16)


def _block_diag(w):
    n, g, i, j = w.shape
    return jnp.einsum('lgij,gh->lgihj', w, jnp.eye(g, dtype=w.dtype)).reshape(n, g * i, g * j)


def kernel(x, meta_tokens, norm_g, ffn1_w_in, ffn1_w_out, ffn2_w_in, ffn2_w_out, mix_w_in, mix_w_out, lru_conv_w, lru_conv_b, lru_wa, lru_ba, lru_wx, lru_bx, lru_lambda, lru_norm_g, sc_conv_w, sc_norm_g, rwkv_mu, rwkv_w0, rwkv_w2, rwkv_a0, rwkv_a2, rwkv_g2, rwkv_k_k, rwkv_k_a, rwkv_r_k, rwkv_lnx_w, rwkv_lnx_b):
    bsz, seq, d = x.shape
    n_meta = meta_tokens.shape[0]
    depth = norm_g.shape[0]
    lru_w = lru_conv_b.shape[1]
    sc_w = sc_norm_g.shape[1]
    rw_w = rwkv_w0.shape[1]
    t_real = n_meta + seq
    t_pad = -(-t_real // RWKV_ROWS) * RWKV_ROWS
    assert lru_w == sc_w and bsz % RWKV_SLOTS == 0
    assert t_real % FFN_ROWS == 0 and seq % OUT_ROWS == 0

    h = jnp.zeros((bsz * t_pad, d), x.dtype)

    ones_bd = _ones_block_diag(lru_w)
    rows3 = lambda p: p.reshape(-1, 1, p.shape[-1])
    f1_in, f1_out, f2_in, f2_out = _bf(ffn1_w_in), _bf(ffn1_w_out), _bf(ffn2_w_in), _bf(ffn2_w_out)
    m_in, m_out = _bf(mix_w_in), _bf(mix_w_out)
    wa_bd, wx_bd = _bf(_block_diag(lru_wa)), _bf(_block_diag(lru_wx))
    n_dl, n_il, n_gl = rwkv_w2.shape[1], rwkv_a2.shape[1], rwkv_g2.shape[1]
    lora_w = jnp.zeros((depth, n_dl + n_il + n_gl, 3 * rw_w), F32)
    lora_w = lora_w.at[:, 0:n_dl, 0:rw_w].set(rwkv_w2)
    lora_w = lora_w.at[:, n_dl:n_dl + n_il, rw_w:2 * rw_w].set(rwkv_a2)
    lora_w = _bf(lora_w.at[:, n_dl + n_il:, 2 * rw_w:].set(rwkv_g2))
    norm_rows = rows3(norm_g)
    n_norms = norm_g.shape[1]
    lru_stacked = (lru_conv_w, rows3(lru_conv_b), wa_bd, rows3(lru_ba), wx_bd, rows3(lru_bx),
                   rows3(lru_lambda), rows3(lru_norm_g), sc_conv_w, rows3(sc_norm_g))
    rw_stacked = (rows3(rwkv_mu), rows3(rwkv_w0), rows3(rwkv_a0), lora_w, rows3(rwkv_k_k),
                  rows3(rwkv_k_a), rows3(rwkv_r_k), rows3(rwkv_lnx_w), rows3(rwkv_lnx_b))

    def row_blocks(block_rows, first, count):
        align = math.gcd(t_pad, first, block_rows)
        per_seq = count // block_rows
        return (block_rows, bsz * per_seq, lambda i: pl.multiple_of(
            (i // per_seq) * t_pad + first + (i % per_seq) * block_rows, align))

    real_rows = row_blocks(FFN_ROWS, 0, t_real)
    out_rows = row_blocks(OUT_ROWS, n_meta, seq)
    per_seq = t_real // FFN_ROWS
    x_align = math.gcd(seq, n_meta, FFN_ROWS)
    embed = (x.reshape(bsz * seq, d), meta_tokens.astype(x.dtype), per_seq,
             lambda i: pl.multiple_of((i // per_seq) * seq
                                      + jnp.maximum((i % per_seq) * FFN_ROWS - n_meta, 0), x_align))

    for l in range(depth):
        g = [_Layer(norm_rows, l * n_norms + i) for i in range(n_norms)]
        h = _ffn(h, g[0], _Layer(f1_in, l), _Layer(f1_out, l), g[1], real_rows,
                 embed=embed if l == 0 else None)
        y_a, y_rw = _mixers(h, g[2], _Layer(m_in, l), [_Layer(p, l) for p in lru_stacked],
                            [_Layer(p, l) for p in rw_stacked], bsz, ones_bd, n_dl, n_il)
        last = l == depth - 1
        h = _ffn(h, g[4], _Layer(f2_in, l), _Layer(f2_out, l), g[5],
                 out_rows if last else real_rows,
                 mix=(y_a, y_rw, _Layer(m_out, l), g[3]), dense_out=last)

    return h.reshape(bsz, seq, d)
```

```python
import functools
import math

import jax
import jax.numpy as jnp
import numpy as np
from jax import lax
from jax.experimental import pallas as pl
from jax.experimental.pallas import tpu as pltpu

F32 = jnp.float32
BF16 = jnp.bfloat16

RMS_EPS = 1e-6
LNX_EPS = 64e-5
LRU_C = 8.0
DECAY_SCALE = math.exp(-0.5)

HEAD = 64
CHUNK = 64
INV_BLOCK = 16
RWKV_ROWS = 192
RWKV_SLOTS = 4
FFN_ROWS = 688
OUT_ROWS = 512
SUBLANES = 8
LANES = 128
PROJ_PIECES = 4
HALO = 8
VMEM_LIMIT = 56 * 1024 * 1024


def _cparams(n_axes):
    return pltpu.CompilerParams(
        dimension_semantics=("arbitrary",) * n_axes, vmem_limit_bytes=VMEM_LIMIT)


class _Layer:
    def __init__(self, stacked, layer):
        self.stacked, self.layer = stacked, layer
        self.shape = stacked.shape[1:]


def _operand(p):
    return p.stacked if isinstance(p, _Layer) else p


def _resident(p):
    zeros = (0,) * len(p.shape)
    if isinstance(p, _Layer):
        layer = p.layer
        return pl.BlockSpec((None,) + tuple(p.shape), lambda *_: (layer,) + zeros,
                            pipeline_mode=pl.Buffered(1))
    return pl.BlockSpec(p.shape, lambda *_: zeros, pipeline_mode=pl.Buffered(1))


def _rms(x, g):
    return x * lax.rsqrt(jnp.mean(x * x, axis=-1, keepdims=True) + RMS_EPS) * g


def _bf(x):
    return x.astype(BF16)


def _mm(a, b):
    return jnp.dot(_bf(a), _bf(b), preferred_element_type=F32)


def _mm_nt(a, b):
    return lax.dot_general(_bf(a), _bf(b), (((1,), (1,)), ((), ())), preferred_element_type=F32)


def _mm_tn(a, b):
    return lax.dot_general(_bf(a), _bf(b), (((0,), (0,)), ((), ())), preferred_element_type=F32)


def _softplus(x):
    return jnp.maximum(x, 0.0) + jnp.log(1.0 + jnp.exp(-jnp.abs(x)))


def _ffn_kernel(*refs, has_mix, embed_blocks):
    if has_mix:
        h_ref, ya_ref, yrw_ref, wm_ref, gm_ref, gpre_ref, wi_ref, wo_ref, gpost_ref, o_ref = refs
    elif embed_blocks:
        _, h_ref, meta_ref, gpre_ref, wi_ref, wo_ref, gpost_ref, o_ref = refs
    else:
        h_ref, gpre_ref, wi_ref, wo_ref, gpost_ref, o_ref = refs
    h = h_ref[...]
    if embed_blocks:
        n_meta = meta_ref.shape[0]
        shifted = jnp.concatenate([meta_ref[...], h[0:h.shape[0] - n_meta]], axis=0)
        h = jnp.where(lax.rem(pl.program_id(0), embed_blocks) == 0, shifted, h)
    if has_mix:
        na = ya_ref.shape[1]
        m = jnp.dot(ya_ref[...], wm_ref[:na, :], preferred_element_type=F32)
        m = m + jnp.dot(yrw_ref[...], wm_ref[na:, :], preferred_element_type=F32)
        h = h + _rms(m, gm_ref[...])
    d_ff = wo_ref.shape[0]
    xn = _bf(_rms(h, gpre_ref[...]))
    gu = jnp.dot(xn, wi_ref[...], preferred_element_type=F32)
    act = _bf(jax.nn.silu(gu[:, :d_ff]) * gu[:, d_ff:])
    y = jnp.dot(act, wo_ref[...], preferred_element_type=F32)
    o_ref[...] = h + 0.5 * _rms(y, gpost_ref[...])


def _ffn(h, g_pre, w_in, w_out, g_post, blocks, mix=None, dense_out=False, embed=None):
    n, d = h.shape
    d_ff = w_out.shape[0]
    block_rows, n_blocks, first_row = blocks
    element_rows = lambda width, start: pl.BlockSpec(
        (pl.Element(block_rows), pl.Element(width)), lambda i: (start(i), 0))
    rows = lambda width: element_rows(width, first_row)
    operands, in_specs = [h], [rows(d)]
    params = (g_pre, w_in, w_out, g_post)
    if embed is not None:
        x, meta, _, x_first_row = embed
        operands += [x, meta]
        in_specs = [pl.BlockSpec(memory_space=pl.ANY), element_rows(d, x_first_row),
                    _resident(meta)]
    if mix is not None:
        y_a, y_rw, w_mix, g_mix = mix
        operands += [y_a, y_rw]
        in_specs += [rows(y_a.shape[1]), rows(y_rw.shape[1])]
        params = (w_mix, g_mix) + params
    operands += [_operand(p) for p in params]
    in_specs += [_resident(p) for p in params]
    if dense_out:
        out_spec = pl.BlockSpec((block_rows, d), lambda i: (i, 0))
        out_shape, aliases = jax.ShapeDtypeStruct((n_blocks * block_rows, d), F32), {}
    else:
        out_spec, out_shape, aliases = rows(d), jax.ShapeDtypeStruct((n, d), F32), {0: 0}
    return pl.pallas_call(
        functools.partial(_ffn_kernel, has_mix=mix is not None,
                          embed_blocks=embed[2] if embed is not None else 0),
        grid=(n_blocks,),
        in_specs=in_specs,
        out_specs=out_spec,
        out_shape=out_shape,
        input_output_aliases=aliases,
        compiler_params=_cparams(1),
        name="ffn_mix" if mix is not None else "ffn",
    )(*operands)


def _group_rms(x, g, ones_bd):
    ms = _mm(x * x, ones_bd) * (1.0 / HEAD)
    return x * lax.rsqrt(ms + RMS_EPS) * g


def _lrusc_stages(p, prm, ones_bd, xh_ref, ch_ref, hs_ref, o_ref):
    cw_ref, cb_ref, wa_ref, ba_ref, wx_ref, bx_ref, lam_ref, lg_ref, scw_ref, scg_ref = prm
    w = cb_ref.shape[1]
    rows = p.shape[0]
    xb = p[:, 0:w]
    gb = p[:, w:2 * w]
    sc_b = p[:, 2 * w:3 * w]
    cx = p[:, 3 * w:4 * w] * p[:, 4 * w:5 * w]

    xx = jnp.concatenate([xh_ref[...], xb], axis=0)
    u = xx[HALO - 3:HALO - 3 + rows] * cw_ref[0:1, :]
    for k in range(1, 4):
        u = u + xx[HALO - 3 + k:HALO - 3 + k + rows] * cw_ref[k:k + 1, :]
    u = u + cb_ref[...]
    xh_ref[...] = xb[rows - HALO:, :]

    yield
    r = jax.nn.sigmoid(_mm(u, wa_ref[...]) + ba_ref[...])
    gi = jax.nn.sigmoid(_mm(u, wx_ref[...]) + bx_ref[...])
    log_a = -LRU_C * r * _softplus(-lam_ref[...])
    a = jnp.exp(log_a)
    th = jnp.tanh(log_a)
    b = jnp.sqrt(-2.0 * th / (1.0 - th)) * (gi * u)

    sub = jnp.bitwise_and(lax.broadcasted_iota(jnp.int32, a.shape, 0), SUBLANES - 1)
    s = 1
    while s < SUBLANES:
        b = a * jnp.where(sub >= s, pltpu.roll(b, s, axis=0), 0.0) + b
        a = a * jnp.where(sub >= s, pltpu.roll(a, s, axis=0), 1.0)
        s *= 2
    carry = hs_ref[...]
    groups = []
    for r0 in range(0, rows, SUBLANES):
        hg = b[r0:r0 + SUBLANES] + a[r0:r0 + SUBLANES] * carry
        groups.append(hg)
        carry = hg[SUBLANES - 1:SUBLANES, :]
    hseq = jnp.concatenate(groups, axis=0)
    hs_ref[...] = carry
    lru_out = jax.nn.gelu(gb) * hseq

    cc = jnp.concatenate([ch_ref[...], cx], axis=0)
    conv = cc[HALO - 2:HALO - 2 + rows] * scw_ref[0:1, :]
    for k in range(1, 3):
        conv = conv + cc[HALO - 2 + k:HALO - 2 + k + rows] * scw_ref[k:k + 1, :]
    ch_ref[...] = cx[rows - HALO:, :]

    yield
    o_ref[:, 0:w] = _group_rms(lru_out, lg_ref[...], ones_bd).astype(o_ref.dtype)
    o_ref[:, w:2 * w] = _group_rms(sc_b * conv, scg_ref[...], ones_bd).astype(o_ref.dtype)


def _pair_bd(x):
    first = lax.broadcasted_iota(jnp.int32, x.shape, 1) < HEAD
    zero = jnp.zeros_like(x)
    return jnp.concatenate([jnp.where(first, x, zero), jnp.where(first, zero, x)], axis=0)


def _solve_unit_lower(lmats, rhss, same_blk, eye):
    dot = lambda a, b: jnp.dot(a, _pair_bd(b), preferred_element_type=F32)
    lds = [jnp.where(same_blk, l, 0.0) for l in lmats]
    los = [_bf(l - ld) for l, ld in zip(lmats, lds)]
    ts = [eye + ld for ld in lds]
    tbs = [_bf(t) for t in ts]
    pbs = [_bf(ld) for ld in lds]
    s = 2
    while s < INV_BLOCK:
        pbs = [_bf(dot(pb, pb)) for pb in pbs]
        ts = [t + dot(tb, pb) for t, tb, pb in zip(ts, tbs, pbs)]
        tbs = [_bf(t) for t in ts]
        s *= 2
    xs = [dot(tb, _bf(r)) for tb, r in zip(tbs, rhss)]
    pbs = [_bf(dot(tb, lo)) for tb, lo in zip(tbs, los)]
    nb = CHUNK // INV_BLOCK
    s = 1
    while s < nb:
        xs = [x + dot(pb, _bf(x)) for x, pb in zip(xs, pbs)]
        s *= 2
        if s < nb:
            pbs = [_bf(dot(pb, pb)) for pb in pbs]
    return xs


def _mixers_kernel(*refs, n_dl, n_il):
    (h_ref, gin_ref, win_ref), lru_prm, refs = refs[:3], refs[3:13], refs[13:]
    (mu_ref, w0_ref, a0_ref, wl_ref, kk_ref, ka_ref, rk_ref, lw_ref, lb_ref, bd_ref,
     ya_ref, o_ref, zl_ref, s_ref, p_ref, xh_ref, ch_ref, hs_ref) = refs
    width = w0_ref.shape[1]
    n_slots = h_ref.shape[0]
    n_chunks = h_ref.shape[1] // CHUNK
    n_rows = n_slots * CHUNK
    ones_bd = bd_ref[...]

    @pl.when(pl.program_id(1) == 0)
    def _():
        for ref in (zl_ref, s_ref, xh_ref, ch_ref, hs_ref):
            ref[...] = jnp.zeros_like(ref)

    t_rows = h_ref.shape[1]
    n_rw = p_ref.shape[2]
    n_a = win_ref.shape[1] - n_rw
    xn = _bf(_rms(h_ref[...].reshape(n_slots * t_rows, h_ref.shape[2]), gin_ref[...]))
    p_a = jnp.dot(xn, win_ref[:, 0:n_a], preferred_element_type=F32)
    pending = [_lrusc_stages(p_a[s * t_rows:(s + 1) * t_rows], lru_prm, ones_bd, xh_ref.at[s],
                             ch_ref.at[s], hs_ref.at[s], ya_ref.at[s]) for s in range(n_slots)]
    piece = -(-n_rw // (PROJ_PIECES * LANES)) * LANES
    for c0 in range(0, n_rw, piece):
        c1 = min(c0 + piece, n_rw)
        p_rw = jnp.dot(xn, win_ref[:, n_a + c0:n_a + c1], preferred_element_type=F32)
        for s in range(n_slots):
            p_ref[s, :, c0:c1] = p_rw[s * t_rows:(s + 1) * t_rows]
        pending = [g for g in pending if next(g, True) is None]
    while pending:
        pending = [g for g in pending if next(g, True) is None]

    pair = 2 * HEAD
    n_pairs = width // pair
    c2 = 2 * CHUNK
    ri = lax.broadcasted_iota(jnp.int32, (c2, 2 * pair), 0)
    cm = jnp.bitwise_and(lax.broadcasted_iota(jnp.int32, (c2, 2 * pair), 1), CHUNK - 1)
    gmask = ((ri < CHUNK) & (cm < ri)) | ((ri >= CHUNK) & (cm <= ri - CHUNK))
    ti = lax.broadcasted_iota(jnp.int32, (CHUNK, pair), 0)
    si = jnp.bitwise_and(lax.broadcasted_iota(jnp.int32, (CHUNK, pair), 1), CHUNK - 1)
    shift = INV_BLOCK.bit_length() - 1
    same_blk = jnp.right_shift(ti, shift) == jnp.right_shift(si, shift)
    eye = (ti == si).astype(F32)
    ti = lax.broadcasted_iota(jnp.int32, (CHUNK, CHUNK), 0)
    tri = (lax.broadcasted_iota(jnp.int32, (CHUNK, CHUNK), 1) <= ti).astype(BF16)
    same_head = ((lax.broadcasted_iota(jnp.int32, (pair, pair), 0) < HEAD)
                 == (lax.broadcasted_iota(jnp.int32, (pair, pair), 1) < HEAD))
    bd_w = ones_bd.shape[0]

    def seg_sum(x):
        xb = _bf(x)
        parts = [jnp.dot(xb[:, c:c + bd_w], ones_bd, preferred_element_type=F32)
                 for c in range(0, x.shape[1], bd_w)]
        return jnp.concatenate(parts, axis=1)

    slots = range(n_slots)
    stack = lambda xs: jnp.concatenate(xs, axis=0)
    unstack = lambda x: [x[s * CHUNK:(s + 1) * CHUNK] for s in slots]

    def chunk_body(c, carry):
        r0 = pl.multiple_of(c * CHUNK, CHUNK)
        zs = [p_ref[s, pl.ds(r0, CHUNK), :] for s in slots]
        first_row = lax.broadcasted_iota(jnp.int32, (SUBLANES, zs[0].shape[1]), 0) < 1
        zz = []
        for s in slots:
            rolled = pltpu.roll(zs[s], 1, axis=0)
            zprev = jnp.concatenate(
                [jnp.where(first_row, zl_ref[s], rolled[0:SUBLANES]), rolled[SUBLANES:]], axis=0)
            zl_ref[s] = zs[s][CHUNK - 1:CHUNK, :]
            zz.append(zs[s] + (zprev - zs[s]) * mu_ref[...])
        zz = stack(zz)
        r = zz[:, 0:width]
        k = zz[:, width:2 * width]
        v = zz[:, 2 * width:3 * width]
        lo = zz[:, 3 * width:]
        lane = lax.broadcasted_iota(jnp.int32, lo.shape, 1)
        lo_act = jnp.where(lane < n_dl, jnp.tanh(lo),
                           jnp.where(lane < n_dl + n_il, lo, jax.nn.sigmoid(lo)))
        lora = _mm(lo_act, wl_ref[...])
        lw = -DECAY_SCALE * jax.nn.sigmoid(w0_ref[...] + lora[:, 0:width])
        a = jax.nn.sigmoid(a0_ref[...] + lora[:, width:2 * width])
        g = lora[:, 2 * width:3 * width]
        kk = k * kk_ref[...]
        kf = k * (1.0 + (a - 1.0) * ka_ref[...])
        sums = seg_sum(jnp.concatenate([kk * kk, r * kf * rk_ref[...]], axis=0))
        kk = kk * lax.rsqrt(jnp.maximum(sums[0:n_rows], 1e-24))
        bonus = sums[n_rows:]
        kka = kk * a

        lw_hi = _bf(lw)
        lw_lo = _bf(lw - lw_hi.astype(F32))
        hi_l = jnp.concatenate(unstack(lw_hi), axis=1)
        lo_l = jnp.concatenate(unstack(lw_lo), axis=1)
        cum_l = (jnp.dot(tri, hi_l, preferred_element_type=F32)
                 + jnp.dot(tri, lo_l, preferred_element_type=F32))
        cum = stack([cum_l[:, s * width:(s + 1) * width] for s in slots])
        w_c = jnp.exp(cum_l[CHUNK - 1:CHUNK, :])
        w_t = jnp.exp(cum)
        w_in = 1.0 / w_t
        w_rem = stack([w * w_c[:, s * width:(s + 1) * width]
                       for s, w in enumerate(unstack(w_in))])
        al = unstack(_bf(-kk * jnp.exp(cum - lw)))
        be = unstack(_bf(kka * w_in))
        kt = unstack(_bf(kf * w_in))
        rt = unstack(_bf(r * w_t))
        bw = unstack(_bf(kka * w_rem))
        kw = unstack(_bf(kf * w_rem))
        vb = unstack(_bf(v))

        chains = [(s, slice(j * pair, (j + 1) * pair)) for s in slots for j in range(n_pairs)]
        cat = jnp.concatenate
        ars = [cat([al[s][:, ps], rt[s][:, ps]], axis=0) for s, ps in chains]
        bks = [cat([_pair_bd(be[s][:, ps]), _pair_bd(kt[s][:, ps])], axis=0) for s, ps in chains]
        bkws = [cat([bw[s][:, ps], kw[s][:, ps]], axis=0) for s, ps in chains]
        vps = [vb[s][:, ps] for s, ps in chains]
        vbds = [_pair_bd(v_p) for v_p in vps]
        wcs = [w_c[:, s * width + j * pair:s * width + (j + 1) * pair]
               for s in slots for j in range(n_pairs)]
        n_chains = n_slots * n_pairs
        s0s = [s_ref[i] for i in range(n_chains)]
        gms = [jnp.where(gmask, _mm_nt(ar, bk), 0.0) for ar, bk in zip(ars, bks)]
        a_ss = [_mm_nt(ar, s0) for ar, s0 in zip(ars, s0s)]
        akvs = [_mm(gm[0:CHUNK, pair:], vbd) for gm, vbd in zip(gms, vbds)]
        rhss = [a_s[0:CHUNK] + akv for a_s, akv in zip(a_ss, akvs)]
        es = _solve_unit_lower([gm[0:CHUNK, 0:pair] for gm in gms], rhss, same_blk, eye)
        ebs = [_bf(e) for e in es]
        ys = [a_s[CHUNK:] + _mm(gm[CHUNK:, :], cat([_pair_bd(eb), vbd], axis=0))
              for a_s, gm, eb, vbd in zip(a_ss, gms, ebs, vbds)]
        for i in range(n_chains):
            upd = _mm_tn(cat([ebs[i], vps[i]], axis=0), bkws[i])
            s_ref[i] = s0s[i] * wcs[i] + jnp.where(same_head, upd, 0.0)

        y = stack([cat(ys[s * n_pairs:(s + 1) * n_pairs], axis=1) for s in slots])
        mean = seg_sum(y) * (1.0 / HEAD)
        d = y - mean
        var = seg_sum(d * d) * (1.0 / HEAD)
        yn = d * lax.rsqrt(var + LNX_EPS) * lw_ref[...] + lb_ref[...]
        out = unstack(((yn + bonus * v) * g).astype(o_ref.dtype))
        for s in slots:
            o_ref[s, pl.ds(r0, CHUNK), :] = out[s]
        return carry

    lax.fori_loop(0, n_chunks, chunk_body, 0)


def _mixers(h, g_in, w_in, lru_params, rw_params, batch, ones_bd, n_dl, n_il):
    n, d = h.shape
    nrw = rw_params[0].shape[1]
    width = rw_params[1].shape[1]
    w_a = lru_params[1].shape[1]
    t_pad = n // batch
    blk = lambda b, t: (b, t, 0)
    params = (g_in, w_in) + tuple(lru_params) + tuple(rw_params) + (ones_bd,)
    out_blk = lambda w: pl.BlockSpec((RWKV_SLOTS, RWKV_ROWS, w), blk)
    y_a, y_rw = pl.pallas_call(
        functools.partial(_mixers_kernel, n_dl=n_dl, n_il=n_il),
        grid=(batch // RWKV_SLOTS, t_pad // RWKV_ROWS),
        in_specs=[pl.BlockSpec((RWKV_SLOTS, RWKV_ROWS, d), blk)] + [_resident(p) for p in params],
        out_specs=[out_blk(2 * w_a), out_blk(width)],
        out_shape=[jax.ShapeDtypeStruct((batch, t_pad, 2 * w_a), BF16),
                   jax.ShapeDtypeStruct((batch, t_pad, width), BF16)],
        scratch_shapes=[pltpu.VMEM((RWKV_SLOTS, 1, nrw), F32),
                        pltpu.VMEM((RWKV_SLOTS * (width // (2 * HEAD)), 2 * HEAD, 2 * HEAD), F32),
                        pltpu.VMEM((RWKV_SLOTS, RWKV_ROWS, nrw), F32),
                        pltpu.VMEM((RWKV_SLOTS, HALO, w_a), F32),
                        pltpu.VMEM((RWKV_SLOTS, HALO, w_a), F32),
                        pltpu.VMEM((RWKV_SLOTS, 1, w_a), F32)],
        compiler_params=_cparams(2),
        name="mixers",
    )(h.reshape(batch, t_pad, d), *[_operand(p) for p in params])
    return y_a.reshape(n, 2 * w_a), y_rw.reshape(n, width)


def _ones_block_diag(width):
    return jnp.asarray(np.kron(np.eye(width // HEAD), np.ones((HEAD, HEAD))), BF16)


def _block_diag(w):
    n, g, i, j = w.shape
    return jnp.einsum('lgij,gh->lgihj', w, jnp.eye(g, dtype=w.dtype)).reshape(n, g * i, g * j)


def kernel(x, meta_tokens, norm_g, ffn1_w_in, ffn1_w_out, ffn2_w_in, ffn2_w_out, mix_w_in, mix_w_out, lru_conv_w, lru_conv_b, lru_wa, lru_ba, lru_wx, lru_bx, lru_lambda, lru_norm_g, sc_conv_w, sc_norm_g, rwkv_mu, rwkv_w0, rwkv_w2, rwkv_a0, rwkv_a2, rwkv_g2, rwkv_k_k, rwkv_k_a, rwkv_r_k, rwkv_lnx_w, rwkv_lnx_b):
    bsz, seq, d = x.shape
    n_meta = meta_tokens.shape[0]
    depth = norm_g.shape[0]
    lru_w = lru_conv_b.shape[1]
    sc_w = sc_norm_g.shape[1]
    rw_w = rwkv_w0.shape[1]
    t_real = n_meta + seq
    t_pad = -(-t_real // RWKV_ROWS) * RWKV_ROWS
    assert lru_w == sc_w and bsz % RWKV_SLOTS == 0
    assert t_real % FFN_ROWS == 0 and seq % OUT_ROWS == 0

    h = jnp.zeros((bsz * t_pad, d), x.dtype)

    ones_bd = _ones_block_diag(lru_w)
    rows3 = lambda p: p.reshape(-1, 1, p.shape[-1])
    f1_in, f1_out, f2_in, f2_out = _bf(ffn1_w_in), _bf(ffn1_w_out), _bf(ffn2_w_in), _bf(ffn2_w_out)
    m_in, m_out = _bf(mix_w_in), _bf(mix_w_out)
    wa_bd, wx_bd = _bf(_block_diag(lru_wa)), _bf(_block_diag(lru_wx))
    n_dl, n_il, n_gl = rwkv_w2.shape[1], rwkv_a2.shape[1], rwkv_g2.shape[1]
    lora_w = jnp.zeros((depth, n_dl + n_il + n_gl, 3 * rw_w), F32)
    lora_w = lora_w.at[:, 0:n_dl, 0:rw_w].set(rwkv_w2)
    lora_w = lora_w.at[:, n_dl:n_dl + n_il, rw_w:2 * rw_w].set(rwkv_a2)
    lora_w = _bf(lora_w.at[:, n_dl + n_il:, 2 * rw_w:].set(rwkv_g2))
    norm_rows = rows3(norm_g)
    n_norms = norm_g.shape[1]
    lru_stacked = (lru_conv_w, rows3(lru_conv_b), wa_bd, rows3(lru_ba), wx_bd, rows3(lru_bx),
                   rows3(lru_lambda), rows3(lru_norm_g), sc_conv_w, rows3(sc_norm_g))
    rw_stacked = (rows3(rwkv_mu), rows3(rwkv_w0), rows3(rwkv_a0), lora_w, rows3(rwkv_k_k),
                  rows3(rwkv_k_a), rows3(rwkv_r_k), rows3(rwkv_lnx_w), rows3(rwkv_lnx_b))

    def row_blocks(block_rows, first, count):
        align = math.gcd(t_pad, first, block_rows)
        per_seq = count // block_rows
        return (block_rows, bsz * per_seq, lambda i: pl.multiple_of(
            (i // per_seq) * t_pad + first + (i % per_seq) * block_rows, align))

    real_rows = row_blocks(FFN_ROWS, 0, t_real)
    out_rows = row_blocks(OUT_ROWS, n_meta, seq)
    per_seq = t_real // FFN_ROWS
    x_align = math.gcd(seq, n_meta, FFN_ROWS)
    embed = (x.reshape(bsz * seq, d), meta_tokens.astype(x.dtype), per_seq,
             lambda i: pl.multiple_of((i // per_seq) * seq
                                      + jnp.maximum((i % per_seq) * FFN_ROWS - n_meta, 0), x_align))

    for l in range(depth):
        g = [_Layer(norm_rows, l * n_norms + i) for i in range(n_norms)]
        h = _ffn(h, g[0], _Layer(f1_in, l), _Layer(f1_out, l), g[1], real_rows,
                 embed=embed if l == 0 else None)
        y_a, y_rw = _mixers(h, g[2], _Layer(m_in, l), [_Layer(p, l) for p in lru_stacked],
                            [_Layer(p, l) for p in rw_stacked], bsz, ones_bd, n_dl, n_il)
        last = l == depth - 1
        h = _ffn(h, g[4], _Layer(f2_in, l), _Layer(f2_out, l), g[5],
                 out_rows if last else real_rows,
                 mix=(y_a, y_rw, _Layer(m_out, l), g[3]), dense_out=last)

    return h.reshape(bsz, seq, d)
```

```python
import functools
import math

import jax
import jax.numpy as jnp
import numpy as np
from jax import lax
from jax.experimental import pallas as pl
from jax.experimental.pallas import tpu as pltpu

F32 = jnp.float32
BF16 = jnp.bfloat16

RMS_EPS = 1e-6
LNX_EPS = 64e-5
LRU_C = 8.0
DECAY_SCALE = math.exp(-0.5)

HEAD = 64
CHUNK = 64
INV_BLOCK = 16
RWKV_ROWS = 192
RWKV_SLOTS = 4
FFN_ROWS = 688
OUT_ROWS = 512
SUBLANES = 8
LANES = 128
PROJ_PIECES = 4
HALO = 8
VMEM_LIMIT = 56 * 1024 * 1024


def _cparams(n_axes):
    return pltpu.CompilerParams(
        dimension_semantics=("arbitrary",) * n_axes, vmem_limit_bytes=VMEM_LIMIT)


class _Layer:
    def __init__(self, stacked, layer):
        self.stacked, self.layer = stacked, layer
        self.shape = stacked.shape[1:]


def _operand(p):
    return p.stacked if isinstance(p, _Layer) else p


def _resident(p):
    zeros = (0,) * len(p.shape)
    if isinstance(p, _Layer):
        layer = p.layer
        return pl.BlockSpec((None,) + tuple(p.shape), lambda *_: (layer,) + zeros,
                            pipeline_mode=pl.Buffered(1))
    return pl.BlockSpec(p.shape, lambda *_: zeros, pipeline_mode=pl.Buffered(1))


def _rms(x, g):
    return x * lax.rsqrt(jnp.mean(x * x, axis=-1, keepdims=True) + RMS_EPS) * g


def _bf(x):
    return x.astype(BF16)


def _mm(a, b):
    return jnp.dot(_bf(a), _bf(b), preferred_element_type=F32)


def _mm_nt(a, b):
    return lax.dot_general(_bf(a), _bf(b), (((1,), (1,)), ((), ())), preferred_element_type=F32)


def _mm_tn(a, b):
    return lax.dot_general(_bf(a), _bf(b), (((0,), (0,)), ((), ())), preferred_element_type=F32)


def _softplus(x):
    return jnp.maximum(x, 0.0) + jnp.log(1.0 + jnp.exp(-jnp.abs(x)))


def _ffn_kernel(*refs, has_mix, embed_blocks):
    if has_mix:
        h_ref, ya_ref, yrw_ref, wm_ref, gm_ref, gpre_ref, wi_ref, wo_ref, gpost_ref, o_ref = refs
    elif embed_blocks:
        _, h_ref, meta_ref, gpre_ref, wi_ref, wo_ref, gpost_ref, o_ref = refs
    else:
        h_ref, gpre_ref, wi_ref, wo_ref, gpost_ref, o_ref = refs
    h = h_ref[...]
    if embed_blocks:
        n_meta = meta_ref.shape[0]
        shifted = jnp.concatenate([meta_ref[...], h[0:h.shape[0] - n_meta]], axis=0)
        h = jnp.where(lax.rem(pl.program_id(0), embed_blocks) == 0, shifted, h)
    if has_mix:
        na = ya_ref.shape[1]
        m = jnp.dot(ya_ref[...], wm_ref[:na, :], preferred_element_type=F32)
        m = m + jnp.dot(yrw_ref[...], wm_ref[na:, :], preferred_element_type=F32)
        h = h + _rms(m, gm_ref[...])
    d_ff = wo_ref.shape[0]
    xn = _bf(_rms(h, gpre_ref[...]))
    gu = jnp.dot(xn, wi_ref[...], preferred_element_type=F32)
    act = _bf(jax.nn.silu(gu[:, :d_ff]) * gu[:, d_ff:])
    y = jnp.dot(act, wo_ref[...], preferred_element_type=F32)
    o_ref[...] = h + 0.5 * _rms(y, gpost_ref[...])


def _ffn(h, g_pre, w_in, w_out, g_post, blocks, mix=None, dense_out=False, embed=None):
    n, d = h.shape
    d_ff = w_out.shape[0]
    block_rows, n_blocks, first_row = blocks
    element_rows = lambda width, start: pl.BlockSpec(
        (pl.Element(block_rows), pl.Element(width)), lambda i: (start(i), 0))
    rows = lambda width: element_rows(width, first_row)
    operands, in_specs = [h], [rows(d)]
    params = (g_pre, w_in, w_out, g_post)
    if embed is not None:
        x, meta, _, x_first_row = embed
        operands += [x, meta]
        in_specs = [pl.BlockSpec(memory_space=pl.ANY), element_rows(d, x_first_row),
                    _resident(meta)]
    if mix is not None:
        y_a, y_rw, w_mix, g_mix = mix
        operands += [y_a, y_rw]
        in_specs += [rows(y_a.shape[1]), rows(y_rw.shape[1])]
        params = (w_mix, g_mix) + params
    operands += [_operand(p) for p in params]
    in_specs += [_resident(p) for p in params]
    if dense_out:
        out_spec = pl.BlockSpec((block_rows, d), lambda i: (i, 0))
        out_shape, aliases = jax.ShapeDtypeStruct((n_blocks * block_rows, d), F32), {}
    else:
        out_spec, out_shape, aliases = rows(d), jax.ShapeDtypeStruct((n, d), F32), {0: 0}
    return pl.pallas_call(
        functools.partial(_ffn_kernel, has_mix=mix is not None,
                          embed_blocks=embed[2] if embed is not None else 0),
        grid=(n_blocks,),
        in_specs=in_specs,
        out_specs=out_spec,
        out_shape=out_shape,
        input_output_aliases=aliases,
        compiler_params=_cparams(1),
        name="ffn_mix" if mix is not None else "ffn",
    )(*operands)


def _group_rms(x, g, ones_bd):
    ms = _mm(x * x, ones_bd) * (1.0 / HEAD)
    return x * lax.rsqrt(ms + RMS_EPS) * g


def _lrusc_stages(p, prm, ones_bd, xh_ref, ch_ref, hs_ref, o_ref):
    cw_ref, cb_ref, wa_ref, ba_ref, wx_ref, bx_ref, lam_ref, lg_ref, scw_ref, scg_ref = prm
    w = cb_ref.shape[1]
    rows = p.shape[0]
    xb = p[:, 0:w]
    gb = p[:, w:2 * w]
    sc_b = p[:, 2 * w:3 * w]
    cx = p[:, 3 * w:4 * w] * p[:, 4 * w:5 * w]

    xx = jnp.concatenate([xh_ref[...], xb], axis=0)
    u = xx[HALO - 3:HALO - 3 + rows] * cw_ref[0:1, :]
    for k in range(1, 4):
        u = u + xx[HALO - 3 + k:HALO - 3 + k + rows] * cw_ref[k:k + 1, :]
    u = u + cb_ref[...]
    xh_ref[...] = xb[rows - HALO:, :]

    yield
    r = jax.nn.sigmoid(_mm(u, wa_ref[...]) + ba_ref[...])
    gi = jax.nn.sigmoid(_mm(u, wx_ref[...]) + bx_ref[...])
    log_a = -LRU_C * r * _softplus(-lam_ref[...])
    a = jnp.exp(log_a)
    th = jnp.tanh(log_a)
    b = jnp.sqrt(-2.0 * th / (1.0 - th)) * (gi * u)

    sub = jnp.bitwise_and(lax.broadcasted_iota(jnp.int32, a.shape, 0), SUBLANES - 1)
    s = 1
    while s < SUBLANES:
        b = a * jnp.where(sub >= s, pltpu.roll(b, s, axis=0), 0.0) + b
        a = a * jnp.where(sub >= s, pltpu.roll(a, s, axis=0), 1.0)
        s *= 2
    carry = hs_ref[...]
    groups = []
    for r0 in range(0, rows, SUBLANES):
        hg = b[r0:r0 + SUBLANES] + a[r0:r0 + SUBLANES] * carry
        groups.append(hg)
        carry = hg[SUBLANES - 1:SUBLANES, :]
    hseq = jnp.concatenate(groups, axis=0)
    hs_ref[...] = carry
    lru_out = jax.nn.gelu(gb) * hseq

    cc = jnp.concatenate([ch_ref[...], cx], axis=0)
    conv = cc[HALO - 2:HALO - 2 + rows] * scw_ref[0:1, :]
    for k in range(1, 3):
        conv = conv + cc[HALO - 2 + k:HALO - 2 + k + rows] * scw_ref[k:k + 1, :]
    ch_ref[...] = cx[rows - HALO:, :]

    yield
    o_ref[:, 0:w] = _group_rms(lru_out, lg_ref[...], ones_bd).astype(o_ref.dtype)
    o_ref[:, w:2 * w] = _group_rms(sc_b * conv, scg_ref[...], ones_bd).astype(o_ref.dtype)


def _pair_bd(x):
    first = lax.broadcasted_iota(jnp.int32, x.shape, 1) < HEAD
    zero = jnp.zeros_like(x)
    return jnp.concatenate([jnp.where(first, x, zero), jnp.where(first, zero, x)], axis=0)


def _solve_unit_lower(lmats, rhss, same_blk, eye):
    dot = lambda a, b: jnp.dot(a, _pair_bd(b), preferred_element_type=F32)
    lds = [jnp.where(same_blk, l, 0.0) for l in lmats]
    los = [_bf(l - ld) for l, ld in zip(lmats, lds)]
    ts = [eye + ld for ld in lds]
    pbs = [_bf(ld) for ld in lds]
    n, lanes = lmats[0].shape
    pbs = [_bf(dot(pb, pb)) for pb in pbs]
    s = 2
    while s < INV_BLOCK:
        if 2 * s < INV_BLOCK:
            both = [dot(jnp.concatenate([pb, _bf(t)], axis=0), pb) for t, pb in zip(ts, pbs)]
            ts = [t + b[n:] for t, b in zip(ts, both)]
            pbs = [_bf(b[0:n]) for b in both]
        else:
            ts = [t + dot(_bf(t), pb) for t, pb in zip(ts, pbs)]
        s *= 2
    tbs = [_bf(t) for t in ts]

    def dot2(a, b1, b2):
        both = jnp.dot(a, jnp.concatenate([_pair_bd(b1), _pair_bd(b2)], axis=1),
                       preferred_element_type=F32)
        return both[:, 0:lanes], both[:, lanes:]

    both = [dot2(tb, _bf(r), lo) for tb, r, lo in zip(tbs, rhss, los)]
    xs = [b[0] for b in both]
    pbs = [_bf(b[1]) for b in both]
    nb = CHUNK // INV_BLOCK
    s = 1
    while s < nb:
        s *= 2
        if s < nb:
            both = [dot2(pb, _bf(x), pb) for x, pb in zip(xs, pbs)]
            xs = [x + b[0] for x, b in zip(xs, both)]
            pbs = [_bf(b[1]) for b in both]
        else:
            xs = [x + dot(pb, _bf(x)) for x, pb in zip(xs, pbs)]
    return xs


def _mixers_kernel(*refs, n_dl, n_il):
    (h_ref, gin_ref, win_ref), lru_prm, refs = refs[:3], refs[3:13], refs[13:]
    (mu_ref, w0_ref, a0_ref, wl_ref, kk_ref, ka_ref, rk_ref, lw_ref, lb_ref, bd_ref,
     ya_ref, o_ref, zl_ref, s_ref, p_ref, xh_ref, ch_ref, hs_ref) = refs
    width = w0_ref.shape[1]
    n_slots = h_ref.shape[0]
    n_chunks = h_ref.shape[1] // CHUNK
    n_rows = n_slots * CHUNK
    ones_bd = bd_ref[...]

    @pl.when(pl.program_id(1) == 0)
    def _():
        for ref in (zl_ref, s_ref, xh_ref, ch_ref, hs_ref):
            ref[...] = jnp.zeros_like(ref)

    t_rows = h_ref.shape[1]
    n_rw = p_ref.shape[2]
    n_a = win_ref.shape[1] - n_rw
    xn = _bf(_rms(h_ref[...].reshape(n_slots * t_rows, h_ref.shape[2]), gin_ref[...]))
    p_a = jnp.dot(xn, win_ref[:, 0:n_a], preferred_element_type=F32)
    pending = [_lrusc_stages(p_a[s * t_rows:(s + 1) * t_rows], lru_prm, ones_bd, xh_ref.at[s],
                             ch_ref.at[s], hs_ref.at[s], ya_ref.at[s]) for s in range(n_slots)]
    piece = -(-n_rw // (PROJ_PIECES * LANES)) * LANES
    for c0 in range(0, n_rw, piece):
        c1 = min(c0 + piece, n_rw)
        p_rw = jnp.dot(xn, win_ref[:, n_a + c0:n_a + c1], preferred_element_type=F32)
        for s in range(n_slots):
            p_ref[s, :, c0:c1] = p_rw[s * t_rows:(s + 1) * t_rows]
        pending = [g for g in pending if next(g, True) is None]
    while pending:
        pending = [g for g in pending if next(g, True) is None]

    pair = 2 * HEAD
    n_pairs = width // pair
    c2 = 2 * CHUNK
    ri = lax.broadcasted_iota(jnp.int32, (c2, 2 * pair), 0)
    cm = jnp.bitwise_and(lax.broadcasted_iota(jnp.int32, (c2, 2 * pair), 1), CHUNK - 1)
    gmask = ((ri < CHUNK) & (cm < ri)) | ((ri >= CHUNK) & (cm <= ri - CHUNK))
    ti = lax.broadcasted_iota(jnp.int32, (CHUNK, pair), 0)
    si = jnp.bitwise_and(lax.broadcasted_iota(jnp.int32, (CHUNK, pair), 1), CHUNK - 1)
    shift = INV_BLOCK.bit_length() - 1
    same_blk = jnp.right_shift(ti, shift) == jnp.right_shift(si, shift)
    eye = (ti == si).astype(F32)
    ti = lax.broadcasted_iota(jnp.int32, (CHUNK, CHUNK), 0)
    tri = (lax.broadcasted_iota(jnp.int32, (CHUNK, CHUNK), 1) <= ti).astype(BF16)
    same_head = ((lax.broadcasted_iota(jnp.int32, (pair, pair), 0) < HEAD)
                 == (lax.broadcasted_iota(jnp.int32, (pair, pair), 1) < HEAD))
    bd_w = ones_bd.shape[0]

    def seg_sum(x):
        xb = _bf(x)
        parts = [jnp.dot(xb[:, c:c + bd_w], ones_bd, preferred_element_type=F32)
                 for c in range(0, x.shape[1], bd_w)]
        return jnp.concatenate(parts, axis=1)

    slots = range(n_slots)
    stack = lambda xs: jnp.concatenate(xs, axis=0)
    unstack = lambda x: [x[s * CHUNK:(s + 1) * CHUNK] for s in slots]

    def chunk_body(c, carry):
        r0 = pl.multiple_of(c * CHUNK, CHUNK)
        zs = [p_ref[s, pl.ds(r0, CHUNK), :] for s in slots]
        first_row = lax.broadcasted_iota(jnp.int32, (SUBLANES, zs[0].shape[1]), 0) < 1
        zz = []
        for s in slots:
            rolled = pltpu.roll(zs[s], 1, axis=0)
            zprev = jnp.concatenate(
                [jnp.where(first_row, zl_ref[s], rolled[0:SUBLANES]), rolled[SUBLANES:]], axis=0)
            zl_ref[s] = zs[s][CHUNK - 1:CHUNK, :]
            zz.append(zs[s] + (zprev - zs[s]) * mu_ref[...])
        zz = stack(zz)
        r = zz[:, 0:width]
        k = zz[:, width:2 * width]
        v = zz[:, 2 * width:3 * width]
        lo = zz[:, 3 * width:]
        lane = lax.broadcasted_iota(jnp.int32, lo.shape, 1)
        lo_act = jnp.where(lane < n_dl, jnp.tanh(lo),
                           jnp.where(lane < n_dl + n_il, lo, jax.nn.sigmoid(lo)))
        lora = _mm(lo_act, wl_ref[...])
        lw = -DECAY_SCALE * jax.nn.sigmoid(w0_ref[...] + lora[:, 0:width])
        a = jax.nn.sigmoid(a0_ref[...] + lora[:, width:2 * width])
        g = lora[:, 2 * width:3 * width]
        kk = k * kk_ref[...]
        kf = k * (1.0 + (a - 1.0) * ka_ref[...])
        sums = seg_sum(jnp.concatenate([kk * kk, r * kf * rk_ref[...]], axis=0))
        kk = kk * lax.rsqrt(jnp.maximum(sums[0:n_rows], 1e-24))
        bonus = sums[n_rows:]
        kka = kk * a

        lw_hi = _bf(lw)
        lw_lo = _bf(lw - lw_hi.astype(F32))
        hi_l = jnp.concatenate(unstack(lw_hi), axis=1)
        lo_l = jnp.concatenate(unstack(lw_lo), axis=1)
        cum_l = (jnp.dot(tri, hi_l, preferred_element_type=F32)
                 + jnp.dot(tri, lo_l, preferred_element_type=F32))
        cum = stack([cum_l[:, s * width:(s + 1) * width] for s in slots])
        w_c = jnp.exp(cum_l[CHUNK - 1:CHUNK, :])
        w_t = jnp.exp(cum)
        w_in = 1.0 / w_t
        w_rem = stack([w * w_c[:, s * width:(s + 1) * width]
                       for s, w in enumerate(unstack(w_in))])
        al = unstack(_bf(-kk * jnp.exp(cum - lw)))
        be = unstack(_bf(kka * w_in))
        kt = unstack(_bf(kf * w_in))
        rt = unstack(_bf(r * w_t))
        bw = unstack(_bf(kka * w_rem))
        kw = unstack(_bf(kf * w_rem))
        vb = unstack(_bf(v))

        chains = [(s, slice(j * pair, (j + 1) * pair)) for s in slots for j in range(n_pairs)]
        cat = jnp.concatenate
        ars = [cat([al[s][:, ps], rt[s][:, ps]], axis=0) for s, ps in chains]
        bks = [cat([_pair_bd(be[s][:, ps]), _pair_bd(kt[s][:, ps])], axis=0) for s, ps in chains]
        bkws = [cat([bw[s][:, ps], kw[s][:, ps]], axis=0) for s, ps in chains]
        vps = [vb[s][:, ps] for s, ps in chains]
        vbds = [_pair_bd(v_p) for v_p in vps]
        wcs = [w_c[:, s * width + j * pair:s * width + (j + 1) * pair]
               for s in slots for j in range(n_pairs)]
        n_chains = n_slots * n_pairs
        s0s = [s_ref[i] for i in range(n_chains)]
        gas = [_mm_nt(ar, cat([bk, _bf(s0)], axis=0)) for ar, bk, s0 in zip(ars, bks, s0s)]
        gms = [jnp.where(gmask, ga[:, 0:2 * pair], 0.0) for ga in gas]
        a_ss = [ga[:, 2 * pair:] for ga in gas]
        akvs = [_mm(gm[0:CHUNK, pair:], vbd) for gm, vbd in zip(gms, vbds)]
        rhss = [a_s[0:CHUNK] + akv for a_s, akv in zip(a_ss, akvs)]
        es = _solve_unit_lower([gm[0:CHUNK, 0:pair] for gm in gms], rhss, same_blk, eye)
        ebs = [_bf(e) for e in es]
        ys = [a_s[CHUNK:] + _mm(gm[CHUNK:, :], cat([_pair_bd(eb), vbd], axis=0))
              for a_s, gm, eb, vbd in zip(a_ss, gms, ebs, vbds)]
        for i in range(n_chains):
            upd = _mm_tn(cat([ebs[i], vps[i]], axis=0), bkws[i])
            s_ref[i] = s0s[i] * wcs[i] + jnp.where(same_head, upd, 0.0)

        y = stack([cat(ys[s * n_pairs:(s + 1) * n_pairs], axis=1) for s in slots])
        mean = seg_sum(y) * (1.0 / HEAD)
        d = y - mean
        var = seg_sum(d * d) * (1.0 / HEAD)
        yn = d * lax.rsqrt(var + LNX_EPS) * lw_ref[...] + lb_ref[...]
        out = unstack(((yn + bonus * v) * g).astype(o_ref.dtype))
        for s in slots:
            o_ref[s, pl.ds(r0, CHUNK), :] = out[s]
        return carry

    lax.fori_loop(0, n_chunks, chunk_body, 0)


def _mixers(h, g_in, w_in, lru_params, rw_params, batch, ones_bd, n_dl, n_il):
    n, d = h.shape
    nrw = rw_params[0].shape[1]
    width = rw_params[1].shape[1]
    w_a = lru_params[1].shape[1]
    t_pad = n // batch
    blk = lambda b, t: (b, t, 0)
    params = (g_in, w_in) + tuple(lru_params) + tuple(rw_params) + (ones_bd,)
    out_blk = lambda w: pl.BlockSpec((RWKV_SLOTS, RWKV_ROWS, w), blk)
    y_a, y_rw = pl.pallas_call(
        functools.partial(_mixers_kernel, n_dl=n_dl, n_il=n_il),
        grid=(batch // RWKV_SLOTS, t_pad // RWKV_ROWS),
        in_specs=[pl.BlockSpec((RWKV_SLOTS, RWKV_ROWS, d), blk)] + [_resident(p) for p in params],
        out_specs=[out_blk(2 * w_a), out_blk(width)],
        out_shape=[jax.ShapeDtypeStruct((batch, t_pad, 2 * w_a), BF16),
                   jax.ShapeDtypeStruct((batch, t_pad, width), BF16)],
        scratch_shapes=[pltpu.VMEM((RWKV_SLOTS, 1, nrw), F32),
                        pltpu.VMEM((RWKV_SLOTS * (width // (2 * HEAD)), 2 * HEAD, 2 * HEAD), F32),
                        pltpu.VMEM((RWKV_SLOTS, RWKV_ROWS, nrw), F32),
                        pltpu.VMEM((RWKV_SLOTS, HALO, w_a), F32),
                        pltpu.VMEM((RWKV_SLOTS, HALO, w_a), F32),
                        pltpu.VMEM((RWKV_SLOTS, 1, w_a), F32)],
        compiler_params=_cparams(2),
        name="mixers",
    )(h.reshape(batch, t_pad, d), *[_operand(p) for p in params])
    return y_a.reshape(n, 2 * w_a), y_rw.reshape(n, width)


def _ones_block_diag(width):
    return jnp.asarray(np.kron(np.eye(width // HEAD), np.ones((HEAD, HEAD))), BF16)


def _block_diag(w):
    n, g, i, j = w.shape
    return jnp.einsum('lgij,gh->lgihj', w, jnp.eye(g, dtype=w.dtype)).reshape(n, g * i, g * j)


def kernel(x, meta_tokens, norm_g, ffn1_w_in, ffn1_w_out, ffn2_w_in, ffn2_w_out, mix_w_in, mix_w_out, lru_conv_w, lru_conv_b, lru_wa, lru_ba, lru_wx, lru_bx, lru_lambda, lru_norm_g, sc_conv_w, sc_norm_g, rwkv_mu, rwkv_w0, rwkv_w2, rwkv_a0, rwkv_a2, rwkv_g2, rwkv_k_k, rwkv_k_a, rwkv_r_k, rwkv_lnx_w, rwkv_lnx_b):
    bsz, seq, d = x.shape
    n_meta = meta_tokens.shape[0]
    depth = norm_g.shape[0]
    lru_w = lru_conv_b.shape[1]
    sc_w = sc_norm_g.shape[1]
    rw_w = rwkv_w0.shape[1]
    t_real = n_meta + seq
    t_pad = -(-t_real // RWKV_ROWS) * RWKV_ROWS
    assert lru_w == sc_w and bsz % RWKV_SLOTS == 0
    assert t_real % FFN_ROWS == 0 and seq % OUT_ROWS == 0

    h = jnp.zeros((bsz * t_pad, d), x.dtype)

    ones_bd = _ones_block_diag(lru_w)
    rows3 = lambda p: p.reshape(-1, 1, p.shape[-1])
    f1_in, f1_out, f2_in, f2_out = _bf(ffn1_w_in), _bf(ffn1_w_out), _bf(ffn2_w_in), _bf(ffn2_w_out)
    m_in, m_out = _bf(mix_w_in), _bf(mix_w_out)
    wa_bd, wx_bd = _bf(_block_diag(lru_wa)), _bf(_block_diag(lru_wx))
    n_dl, n_il, n_gl = rwkv_w2.shape[1], rwkv_a2.shape[1], rwkv_g2.shape[1]
    lora_w = jnp.zeros((depth, n_dl + n_il + n_gl, 3 * rw_w), F32)
    lora_w = lora_w.at[:, 0:n_dl, 0:rw_w].set(rwkv_w2)
    lora_w = lora_w.at[:, n_dl:n_dl + n_il, rw_w:2 * rw_w].set(rwkv_a2)
    lora_w = _bf(lora_w.at[:, n_dl + n_il:, 2 * rw_w:].set(rwkv_g2))
    norm_rows = rows3(norm_g)
    n_norms = norm_g.shape[1]
    lru_stacked = (lru_conv_w, rows3(lru_conv_b), wa_bd, rows3(lru_ba), wx_bd, rows3(lru_bx),
                   rows3(lru_lambda), rows3(lru_norm_g), sc_conv_w, rows3(sc_norm_g))
    rw_stacked = (rows3(rwkv_mu), rows3(rwkv_w0), rows3(rwkv_a0), lora_w, rows3(rwkv_k_k),
                  rows3(rwkv_k_a), rows3(rwkv_r_k), rows3(rwkv_lnx_w), rows3(rwkv_lnx_b))

    def row_blocks(block_rows, first, count):
        align = math.gcd(t_pad, first, block_rows)
        per_seq = count // block_rows
        return (block_rows, bsz * per_seq, lambda i: pl.multiple_of(
            (i // per_seq) * t_pad + first + (i % per_seq) * block_rows, align))

    real_rows = row_blocks(FFN_ROWS, 0, t_real)
    out_rows = row_blocks(OUT_ROWS, n_meta, seq)
    per_seq = t_real // FFN_ROWS
    x_align = math.gcd(seq, n_meta, FFN_ROWS)
    embed = (x.reshape(bsz * seq, d), meta_tokens.astype(x.dtype), per_seq,
             lambda i: pl.multiple_of((i // per_seq) * seq
                                      + jnp.maximum((i % per_seq) * FFN_ROWS - n_meta, 0), x_align))

    for l in range(depth):
        g = [_Layer(norm_rows, l * n_norms + i) for i in range(n_norms)]
        h = _ffn(h, g[0], _Layer(f1_in, l), _Layer(f1_out, l), g[1], real_rows,
                 embed=embed if l == 0 else None)
        y_a, y_rw = _mixers(h, g[2], _Layer(m_in, l), [_Layer(p, l) for p in lru_stacked],
                            [_Layer(p, l) for p in rw_stacked], bsz, ones_bd, n_dl, n_il)
        last = l == depth - 1
        h = _ffn(h, g[4], _Layer(f2_in, l), _Layer(f2_out, l), g[5],
                 out_rows if last else real_rows,
                 mix=(y_a, y_rw, _Layer(m_out, l), g[3]), dense_out=last)

    return h.reshape(bsz, seq, d)
```

```python
import functools
import math

import jax
import jax.numpy as jnp
import numpy as np
from jax import lax
from jax.experimental import pallas as pl
from jax.experimental.pallas import tpu as pltpu

F32 = jnp.float32
BF16 = jnp.bfloat16

RMS_EPS = 1e-6
LNX_EPS = 64e-5
LRU_C = 8.0
DECAY_SCALE = math.exp(-0.5)

HEAD = 64
CHUNK = 64
INV_BLOCK = 16
RWKV_ROWS = 192
RWKV_SLOTS = 4
FFN_ROWS = 688
OUT_ROWS = 512
SUBLANES = 8
LANES = 128
PROJ_PIECES = 4
HALO = 8
VMEM_LIMIT = 56 * 1024 * 1024


def _cparams(n_axes):
    return pltpu.CompilerParams(
        dimension_semantics=("arbitrary",) * n_axes, vmem_limit_bytes=VMEM_LIMIT)


class _Layer:
    def __init__(self, stacked, layer):
        self.stacked, self.layer = stacked, layer
        self.shape = stacked.shape[1:]


def _operand(p):
    return p.stacked if isinstance(p, _Layer) else p


def _resident(p):
    zeros = (0,) * len(p.shape)
    if isinstance(p, _Layer):
        layer = p.layer
        return pl.BlockSpec((None,) + tuple(p.shape), lambda *_: (layer,) + zeros,
                            pipeline_mode=pl.Buffered(1))
    return pl.BlockSpec(p.shape, lambda *_: zeros, pipeline_mode=pl.Buffered(1))


def _rms(x, g):
    return x * lax.rsqrt(jnp.mean(x * x, axis=-1, keepdims=True) + RMS_EPS) * g


def _bf(x):
    return x.astype(BF16)


def _mm(a, b):
    return jnp.dot(_bf(a), _bf(b), preferred_element_type=F32)


def _mm_nt(a, b):
    return lax.dot_general(_bf(a), _bf(b), (((1,), (1,)), ((), ())), preferred_element_type=F32)


def _mm_tn(a, b):
    return lax.dot_general(_bf(a), _bf(b), (((0,), (0,)), ((), ())), preferred_element_type=F32)


def _softplus(x):
    return jnp.maximum(x, 0.0) + jnp.log(1.0 + jnp.exp(-jnp.abs(x)))


def _ffn_kernel(*refs, has_mix, embed_blocks):
    if has_mix:
        h_ref, ya_ref, yrw_ref, wm_ref, gm_ref, gpre_ref, wi_ref, wo_ref, gpost_ref, o_ref = refs
    elif embed_blocks:
        _, h_ref, meta_ref, gpre_ref, wi_ref, wo_ref, gpost_ref, o_ref = refs
    else:
        h_ref, gpre_ref, wi_ref, wo_ref, gpost_ref, o_ref = refs
    h = h_ref[...]
    if embed_blocks:
        n_meta = meta_ref.shape[0]
        shifted = jnp.concatenate([meta_ref[...], h[0:h.shape[0] - n_meta]], axis=0)
        h = jnp.where(lax.rem(pl.program_id(0), embed_blocks) == 0, shifted, h)
    if has_mix:
        na = ya_ref.shape[1]
        m = jnp.dot(ya_ref[...], wm_ref[:na, :], preferred_element_type=F32)
        m = m + jnp.dot(yrw_ref[...], wm_ref[na:, :], preferred_element_type=F32)
        h = h + _rms(m, gm_ref[...])
    d_ff = wo_ref.shape[0]
    xn = _bf(_rms(h, gpre_ref[...]))
    gu = jnp.dot(xn, wi_ref[...], preferred_element_type=F32)
    act = _bf(jax.nn.silu(gu[:, :d_ff]) * gu[:, d_ff:])
    y = jnp.dot(act, wo_ref[...], preferred_element_type=F32)
    o_ref[...] = h + 0.5 * _rms(y, gpost_ref[...])


def _ffn(h, g_pre, w_in, w_out, g_post, blocks, mix=None, dense_out=False, embed=None):
    n, d = h.shape
    d_ff = w_out.shape[0]
    block_rows, n_blocks, first_row = blocks
    element_rows = lambda width, start: pl.BlockSpec(
        (pl.Element(block_rows), pl.Element(width)), lambda i: (start(i), 0))
    rows = lambda width: element_rows(width, first_row)
    operands, in_specs = [h], [rows(d)]
    params = (g_pre, w_in, w_out, g_post)
    if embed is not None:
        x, meta, _, x_first_row = embed
        operands += [x, meta]
        in_specs = [pl.BlockSpec(memory_space=pl.ANY), element_rows(d, x_first_row),
                    _resident(meta)]
    if mix is not None:
        y_a, y_rw, w_mix, g_mix = mix
        operands += [y_a, y_rw]
        in_specs += [rows(y_a.shape[1]), rows(y_rw.shape[1])]
        params = (w_mix, g_mix) + params
    operands += [_operand(p) for p in params]
    in_specs += [_resident(p) for p in params]
    if dense_out:
        out_spec = pl.BlockSpec((block_rows, d), lambda i: (i, 0))
        out_shape, aliases = jax.ShapeDtypeStruct((n_blocks * block_rows, d), F32), {}
    else:
        out_spec, out_shape, aliases = rows(d), jax.ShapeDtypeStruct((n, d), F32), {0: 0}
    return pl.pallas_call(
        functools.partial(_ffn_kernel, has_mix=mix is not None,
                          embed_blocks=embed[2] if embed is not None else 0),
        grid=(n_blocks,),
        in_specs=in_specs,
        out_specs=out_spec,
        out_shape=out_shape,
        input_output_aliases=aliases,
        compiler_params=_cparams(1),
        name="ffn_mix" if mix is not None else "ffn",
    )(*operands)


def _group_rms(x, g, ones_bd):
    ms = _mm(x * x, ones_bd) * (1.0 / HEAD)
    return x * lax.rsqrt(ms + RMS_EPS) * g


def _lrusc_stages(p, prm, ones_bd, xh_ref, ch_ref, hs_ref, o_ref):
    cw_ref, cb_ref, wa_ref, ba_ref, wx_ref, bx_ref, lam_ref, lg_ref, scw_ref, scg_ref = prm
    w = cb_ref.shape[1]
    rows = p.shape[0]
    xb = p[:, 0:w]
    gb = p[:, w:2 * w]
    sc_b = p[:, 2 * w:3 * w]
    cx = p[:, 3 * w:4 * w] * p[:, 4 * w:5 * w]

    xx = jnp.concatenate([xh_ref[...], xb], axis=0)
    u = xx[HALO - 3:HALO - 3 + rows] * cw_ref[0:1, :]
    for k in range(1, 4):
        u = u + xx[HALO - 3 + k:HALO - 3 + k + rows] * cw_ref[k:k + 1, :]
    u = u + cb_ref[...]
    xh_ref[...] = xb[rows - HALO:, :]

    yield
    r = jax.nn.sigmoid(_mm(u, wa_ref[...]) + ba_ref[...])
    gi = jax.nn.sigmoid(_mm(u, wx_ref[...]) + bx_ref[...])
    log_a = -LRU_C * r * _softplus(-lam_ref[...])
    a = jnp.exp(log_a)
    th = jnp.tanh(log_a)
    b = jnp.sqrt(-2.0 * th / (1.0 - th)) * (gi * u)

    sub = jnp.bitwise_and(lax.broadcasted_iota(jnp.int32, a.shape, 0), SUBLANES - 1)
    s = 1
    while s < SUBLANES:
        b = a * jnp.where(sub >= s, pltpu.roll(b, s, axis=0), 0.0) + b
        a = a * jnp.where(sub >= s, pltpu.roll(a, s, axis=0), 1.0)
        s *= 2
    carry = hs_ref[...]
    groups = []
    for r0 in range(0, rows, SUBLANES):
        hg = b[r0:r0 + SUBLANES] + a[r0:r0 + SUBLANES] * carry
        groups.append(hg)
        carry = hg[SUBLANES - 1:SUBLANES, :]
    hseq = jnp.concatenate(groups, axis=0)
    hs_ref[...] = carry
    lru_out = jax.nn.gelu(gb) * hseq

    cc = jnp.concatenate([ch_ref[...], cx], axis=0)
    conv = cc[HALO - 2:HALO - 2 + rows] * scw_ref[0:1, :]
    for k in range(1, 3):
        conv = conv + cc[HALO - 2 + k:HALO - 2 + k + rows] * scw_ref[k:k + 1, :]
    ch_ref[...] = cx[rows - HALO:, :]

    yield
    o_ref[:, 0:w] = _group_rms(lru_out, lg_ref[...], ones_bd).astype(o_ref.dtype)
    o_ref[:, w:2 * w] = _group_rms(sc_b * conv, scg_ref[...], ones_bd).astype(o_ref.dtype)


def _pair_bd(x):
    first = lax.broadcasted_iota(jnp.int32, x.shape, 1) < HEAD
    zero = jnp.zeros_like(x)
    return jnp.concatenate([jnp.where(first, x, zero), jnp.where(first, zero, x)], axis=0)


def _solve_unit_lower(lmats, rhss, same_blk, eye):
    dot = lambda a, b: jnp.dot(a, _pair_bd(b), preferred_element_type=F32)
    lds = [jnp.where(same_blk, l, 0.0) for l in lmats]
    los = [_bf(l - ld) for l, ld in zip(lmats, lds)]
    ts = [eye + ld for ld in lds]
    pbs = [_bf(ld) for ld in lds]
    n, lanes = lmats[0].shape
    pbs = [_bf(dot(pb, pb)) for pb in pbs]
    s = 2
    while s < INV_BLOCK:
        if 2 * s < INV_BLOCK:
            both = [dot(jnp.concatenate([pb, _bf(t)], axis=0), pb) for t, pb in zip(ts, pbs)]
            ts = [t + b[n:] for t, b in zip(ts, both)]
            pbs = [_bf(b[0:n]) for b in both]
        else:
            ts = [t + dot(_bf(t), pb) for t, pb in zip(ts, pbs)]
        s *= 2
    tbs = [_bf(t) for t in ts]

    def dot2(a, b1, b2):
        both = jnp.dot(a, jnp.concatenate([_pair_bd(b1), _pair_bd(b2)], axis=1),
                       preferred_element_type=F32)
        return both[:, 0:lanes], both[:, lanes:]

    both = [dot2(tb, _bf(r), lo) for tb, r, lo in zip(tbs, rhss, los)]
    xs = [b[0] for b in both]
    pbs = [_bf(b[1]) for b in both]
    nb = CHUNK // INV_BLOCK
    s = 1
    while s < nb:
        s *= 2
        if s < nb:
            both = [dot2(pb, _bf(x), pb) for x, pb in zip(xs, pbs)]
            xs = [x + b[0] for x, b in zip(xs, both)]
            pbs = [_bf(b[1]) for b in both]
        else:
            xs = [x + dot(pb, _bf(x)) for x, pb in zip(xs, pbs)]
    return xs


def _mixers_kernel(*refs, n_dl, n_il):
    (h_ref, gin_ref, win_ref), lru_prm, refs = refs[:3], refs[3:13], refs[13:]
    (mu_ref, w0_ref, a0_ref, wl_ref, kk_ref, ka_ref, rk_ref, lw_ref, lb_ref, bd_ref,
     ya_ref, o_ref, zl_ref, s_ref, p_ref, xh_ref, ch_ref, hs_ref) = refs
    width = w0_ref.shape[1]
    n_slots = h_ref.shape[0]
    n_chunks = h_ref.shape[1] // CHUNK
    n_rows = n_slots * CHUNK
    ones_bd = bd_ref[...]

    @pl.when(pl.program_id(1) == 0)
    def _():
        for ref in (zl_ref, s_ref, xh_ref, ch_ref, hs_ref):
            ref[...] = jnp.zeros_like(ref)

    t_rows = h_ref.shape[1]
    n_rw = p_ref.shape[2]
    n_a = win_ref.shape[1] - n_rw
    xn = _bf(_rms(h_ref[...].reshape(n_slots * t_rows, h_ref.shape[2]), gin_ref[...]))
    p_a = jnp.dot(xn, win_ref[:, 0:n_a], preferred_element_type=F32)
    pending = [_lrusc_stages(p_a[s * t_rows:(s + 1) * t_rows], lru_prm, ones_bd, xh_ref.at[s],
                             ch_ref.at[s], hs_ref.at[s], ya_ref.at[s]) for s in range(n_slots)]
    piece = -(-n_rw // (PROJ_PIECES * LANES)) * LANES
    for c0 in range(0, n_rw, piece):
        c1 = min(c0 + piece, n_rw)
        p_rw = jnp.dot(xn, win_ref[:, n_a + c0:n_a + c1], preferred_element_type=F32)
        first_row = lax.broadcasted_iota(jnp.int32, (SUBLANES, c1 - c0), 0) < 1
        for s in range(n_slots):
            z = p_rw[s * t_rows:(s + 1) * t_rows]
            rolled = pltpu.roll(z, 1, axis=0)
            zprev = jnp.concatenate([jnp.where(first_row, zl_ref[s, :, c0:c1], rolled[0:SUBLANES]),
                                     rolled[SUBLANES:]], axis=0)
            zl_ref[s, :, c0:c1] = z[t_rows - 1:t_rows, :]
            p_ref[s, :, c0:c1] = z + (zprev - z) * mu_ref[:, c0:c1]
        pending = [g for g in pending if next(g, True) is None]
    while pending:
        pending = [g for g in pending if next(g, True) is None]

    pair = 2 * HEAD
    n_pairs = width // pair
    c2 = 2 * CHUNK
    ri = lax.broadcasted_iota(jnp.int32, (c2, 2 * pair), 0)
    cm = jnp.bitwise_and(lax.broadcasted_iota(jnp.int32, (c2, 2 * pair), 1), CHUNK - 1)
    gmask = ((ri < CHUNK) & (cm < ri)) | ((ri >= CHUNK) & (cm <= ri - CHUNK))
    ti = lax.broadcasted_iota(jnp.int32, (CHUNK, pair), 0)
    si = jnp.bitwise_and(lax.broadcasted_iota(jnp.int32, (CHUNK, pair), 1), CHUNK - 1)
    shift = INV_BLOCK.bit_length() - 1
    same_blk = jnp.right_shift(ti, shift) == jnp.right_shift(si, shift)
    eye = (ti == si).astype(F32)
    ti = lax.broadcasted_iota(jnp.int32, (CHUNK, CHUNK), 0)
    tri = (lax.broadcasted_iota(jnp.int32, (CHUNK, CHUNK), 1) <= ti).astype(BF16)
    same_head = ((lax.broadcasted_iota(jnp.int32, (pair, pair), 0) < HEAD)
                 == (lax.broadcasted_iota(jnp.int32, (pair, pair), 1) < HEAD))
    bd_w = ones_bd.shape[0]

    def seg_sum(x):
        xb = _bf(x)
        parts = [jnp.dot(xb[:, c:c + bd_w], ones_bd, preferred_element_type=F32)
                 for c in range(0, x.shape[1], bd_w)]
        return jnp.concatenate(parts, axis=1)

    slots = range(n_slots)
    stack = lambda xs: jnp.concatenate(xs, axis=0)
    unstack = lambda x: [x[s * CHUNK:(s + 1) * CHUNK] for s in slots]

    def chunk_body(c, carry):
        r0 = pl.multiple_of(c * CHUNK, CHUNK)
        zz = stack([p_ref[s, pl.ds(r0, CHUNK), :] for s in slots])
        r = zz[:, 0:width]
        k = zz[:, width:2 * width]
        v = zz[:, 2 * width:3 * width]
        lo = zz[:, 3 * width:]
        lane = lax.broadcasted_iota(jnp.int32, lo.shape, 1)
        lo_act = jnp.where(lane < n_dl, jnp.tanh(lo),
                           jnp.where(lane < n_dl + n_il, lo, jax.nn.sigmoid(lo)))
        lora = _mm(lo_act, wl_ref[...])
        lw = -DECAY_SCALE * jax.nn.sigmoid(w0_ref[...] + lora[:, 0:width])
        a = jax.nn.sigmoid(a0_ref[...] + lora[:, width:2 * width])
        g = lora[:, 2 * width:3 * width]
        kk = k * kk_ref[...]
        kf = k * (1.0 + (a - 1.0) * ka_ref[...])
        sums = seg_sum(jnp.concatenate([kk * kk, r * kf * rk_ref[...]], axis=0))
        kk = kk * lax.rsqrt(jnp.maximum(sums[0:n_rows], 1e-24))
        bonus = sums[n_rows:]
        kka = kk * a

        lw_hi = _bf(lw)
        lw_lo = _bf(lw - lw_hi.astype(F32))
        hi_l = jnp.concatenate(unstack(lw_hi), axis=1)
        lo_l = jnp.concatenate(unstack(lw_lo), axis=1)
        cum_l = (jnp.dot(tri, hi_l, preferred_element_type=F32)
                 + jnp.dot(tri, lo_l, preferred_element_type=F32))
        cum = stack([cum_l[:, s * width:(s + 1) * width] for s in slots])
        w_c = jnp.exp(cum_l[CHUNK - 1:CHUNK, :])
        w_t = jnp.exp(cum)
        w_in = 1.0 / w_t
        w_rem = stack([w * w_c[:, s * width:(s + 1) * width]
                       for s, w in enumerate(unstack(w_in))])
        al = unstack(_bf(-kk * jnp.exp(cum - lw)))
        be = unstack(_bf(kka * w_in))
        kt = unstack(_bf(kf * w_in))
        rt = unstack(_bf(r * w_t))
        bw = unstack(_bf(kka * w_rem))
        kw = unstack(_bf(kf * w_rem))
        vb = unstack(_bf(v))

        chains = [(s, slice(j * pair, (j + 1) * pair)) for s in slots for j in range(n_pairs)]
        cat = jnp.concatenate
        ars = [cat([al[s][:, ps], rt[s][:, ps]], axis=0) for s, ps in chains]
        bks = [cat([_pair_bd(be[s][:, ps]), _pair_bd(kt[s][:, ps])], axis=0) for s, ps in chains]
        bkws = [cat([bw[s][:, ps], kw[s][:, ps]], axis=0) for s, ps in chains]
        vps = [vb[s][:, ps] for s, ps in chains]
        vbds = [_pair_bd(v_p) for v_p in vps]
        wcs = [w_c[:, s * width + j * pair:s * width + (j + 1) * pair]
               for s in slots for j in range(n_pairs)]
        n_chains = n_slots * n_pairs
        s0s = [s_ref[i] for i in range(n_chains)]
        gas = [_mm_nt(ar, cat([bk, _bf(s0)], axis=0)) for ar, bk, s0 in zip(ars, bks, s0s)]
        gms = [jnp.where(gmask, ga[:, 0:2 * pair], 0.0) for ga in gas]
        a_ss = [ga[:, 2 * pair:] for ga in gas]
        akvs = [_mm(gm[0:CHUNK, pair:], vbd) for gm, vbd in zip(gms, vbds)]
        rhss = [a_s[0:CHUNK] + akv for a_s, akv in zip(a_ss, akvs)]
        es = _solve_unit_lower([gm[0:CHUNK, 0:pair] for gm in gms], rhss, same_blk, eye)
        ebs = [_bf(e) for e in es]
        ys = [a_s[CHUNK:] + _mm(gm[CHUNK:, :], cat([_pair_bd(eb), vbd], axis=0))
              for a_s, gm, eb, vbd in zip(a_ss, gms, ebs, vbds)]
        for i in range(n_chains):
            upd = _mm_tn(cat([ebs[i], vps[i]], axis=0), bkws[i])
            s_ref[i] = s0s[i] * wcs[i] + jnp.where(same_head, upd, 0.0)

        y = stack([cat(ys[s * n_pairs:(s + 1) * n_pairs], axis=1) for s in slots])
        mean = seg_sum(y) * (1.0 / HEAD)
        d = y - mean
        var = seg_sum(d * d) * (1.0 / HEAD)
        yn = d * lax.rsqrt(var + LNX_EPS) * lw_ref[...] + lb_ref[...]
        out = unstack(((yn + bonus * v) * g).astype(o_ref.dtype))
        for s in slots:
            o_ref[s, pl.ds(r0, CHUNK), :] = out[s]
        return carry

    lax.fori_loop(0, n_chunks, chunk_body, 0)


def _mixers(h, g_in, w_in, lru_params, rw_params, batch, ones_bd, n_dl, n_il):
    n, d = h.shape
    nrw = rw_params[0].shape[1]
    width = rw_params[1].shape[1]
    w_a = lru_params[1].shape[1]
    t_pad = n // batch
    blk = lambda b, t: (b, t, 0)
    params = (g_in, w_in) + tuple(lru_params) + tuple(rw_params) + (ones_bd,)
    out_blk = lambda w: pl.BlockSpec((RWKV_SLOTS, RWKV_ROWS, w), blk)
    y_a, y_rw = pl.pallas_call(
        functools.partial(_mixers_kernel, n_dl=n_dl, n_il=n_il),
        grid=(batch // RWKV_SLOTS, t_pad // RWKV_ROWS),
        in_specs=[pl.BlockSpec((RWKV_SLOTS, RWKV_ROWS, d), blk)] + [_resident(p) for p in params],
        out_specs=[out_blk(2 * w_a), out_blk(width)],
        out_shape=[jax.ShapeDtypeStruct((batch, t_pad, 2 * w_a), BF16),
                   jax.ShapeDtypeStruct((batch, t_pad, width), BF16)],
        scratch_shapes=[pltpu.VMEM((RWKV_SLOTS, 1, nrw), F32),
                        pltpu.VMEM((RWKV_SLOTS * (width // (2 * HEAD)), 2 * HEAD, 2 * HEAD), F32),
                        pltpu.VMEM((RWKV_SLOTS, RWKV_ROWS, nrw), F32),
                        pltpu.VMEM((RWKV_SLOTS, HALO, w_a), F32),
                        pltpu.VMEM((RWKV_SLOTS, HALO, w_a), F32),
                        pltpu.VMEM((RWKV_SLOTS, 1, w_a), F32)],
        compiler_params=_cparams(2),
        name="mixers",
    )(h.reshape(batch, t_pad, d), *[_operand(p) for p in params])
    return y_a.reshape(n, 2 * w_a), y_rw.reshape(n, width)


def _ones_block_diag(width):
    return jnp.asarray(np.kron(np.eye(width // HEAD), np.ones((HEAD, HEAD))), BF16)


def _block_diag(w):
    n, g, i, j = w.shape
    return jnp.einsum('lgij,gh->lgihj', w, jnp.eye(g, dtype=w.dtype)).reshape(n, g * i, g * j)


def kernel(x, meta_tokens, norm_g, ffn1_w_in, ffn1_w_out, ffn2_w_in, ffn2_w_out, mix_w_in, mix_w_out, lru_conv_w, lru_conv_b, lru_wa, lru_ba, lru_wx, lru_bx, lru_lambda, lru_norm_g, sc_conv_w, sc_norm_g, rwkv_mu, rwkv_w0, rwkv_w2, rwkv_a0, rwkv_a2, rwkv_g2, rwkv_k_k, rwkv_k_a, rwkv_r_k, rwkv_lnx_w, rwkv_lnx_b):
    bsz, seq, d = x.shape
    n_meta = meta_tokens.shape[0]
    depth = norm_g.shape[0]
    lru_w = lru_conv_b.shape[1]
    sc_w = sc_norm_g.shape[1]
    rw_w = rwkv_w0.shape[1]
    t_real = n_meta + seq
    t_pad = -(-t_real // RWKV_ROWS) * RWKV_ROWS
    assert lru_w == sc_w and bsz % RWKV_SLOTS == 0
    assert t_real % FFN_ROWS == 0 and seq % OUT_ROWS == 0

    h = jnp.zeros((bsz * t_pad, d), x.dtype)

    ones_bd = _ones_block_diag(lru_w)
    rows3 = lambda p: p.reshape(-1, 1, p.shape[-1])
    f1_in, f1_out, f2_in, f2_out = _bf(ffn1_w_in), _bf(ffn1_w_out), _bf(ffn2_w_in), _bf(ffn2_w_out)
    m_in, m_out = _bf(mix_w_in), _bf(mix_w_out)
    wa_bd, wx_bd = _bf(_block_diag(lru_wa)), _bf(_block_diag(lru_wx))
    n_dl, n_il, n_gl = rwkv_w2.shape[1], rwkv_a2.shape[1], rwkv_g2.shape[1]
    lora_w = jnp.zeros((depth, n_dl + n_il + n_gl, 3 * rw_w), F32)
    lora_w = lora_w.at[:, 0:n_dl, 0:rw_w].set(rwkv_w2)
    lora_w = lora_w.at[:, n_dl:n_dl + n_il, rw_w:2 * rw_w].set(rwkv_a2)
    lora_w = _bf(lora_w.at[:, n_dl + n_il:, 2 * rw_w:].set(rwkv_g2))
    norm_rows = rows3(norm_g)
    n_norms = norm_g.shape[1]
    lru_stacked = (lru_conv_w, rows3(lru_conv_b), wa_bd, rows3(lru_ba), wx_bd, rows3(lru_bx),
                   rows3(lru_lambda), rows3(lru_norm_g), sc_conv_w, rows3(sc_norm_g))
    rw_stacked = (rows3(rwkv_mu), rows3(rwkv_w0), rows3(rwkv_a0), lora_w, rows3(rwkv_k_k),
                  rows3(rwkv_k_a), rows3(rwkv_r_k), rows3(rwkv_lnx_w), rows3(rwkv_lnx_b))

    def row_blocks(block_rows, first, count):
        align = math.gcd(t_pad, first, block_rows)
        per_seq = count // block_rows
        return (block_rows, bsz * per_seq, lambda i: pl.multiple_of(
            (i // per_seq) * t_pad + first + (i % per_seq) * block_rows, align))

    real_rows = row_blocks(FFN_ROWS, 0, t_real)
    out_rows = row_blocks(OUT_ROWS, n_meta, seq)
    per_seq = t_real // FFN_ROWS
    x_align = math.gcd(seq, n_meta, FFN_ROWS)
    embed = (x.reshape(bsz * seq, d), meta_tokens.astype(x.dtype), per_seq,
             lambda i: pl.multiple_of((i // per_seq) * seq
                                      + jnp.maximum((i % per_seq) * FFN_ROWS - n_meta, 0), x_align))

    for l in range(depth):
        g = [_Layer(norm_rows, l * n_norms + i) for i in range(n_norms)]
        h = _ffn(h, g[0], _Layer(f1_in, l), _Layer(f1_out, l), g[1], real_rows,
                 embed=embed if l == 0 else None)
        y_a, y_rw = _mixers(h, g[2], _Layer(m_in, l), [_Layer(p, l) for p in lru_stacked],
                            [_Layer(p, l) for p in rw_stacked], bsz, ones_bd, n_dl, n_il)
        last = l == depth - 1
        h = _ffn(h, g[4], _Layer(f2_in, l), _Layer(f2_out, l), g[5],
                 out_rows if last else real_rows,
                 mix=(y_a, y_rw, _Layer(m_out, l), g[3]), dense_out=last)

    return h.reshape(bsz, seq, d)
```

```python
import functools
import math

import jax
import jax.numpy as jnp
import numpy as np
from jax import lax
from jax.experimental import pallas as pl
from jax.experimental.pallas import tpu as pltpu

F32 = jnp.float32
BF16 = jnp.bfloat16

RMS_EPS = 1e-6
LNX_EPS = 64e-5
LRU_C = 8.0
DECAY_SCALE = math.exp(-0.5)

HEAD = 64
CHUNK = 64
INV_BLOCK = 16
RWKV_ROWS = 192
RWKV_SLOTS = 4
FFN_ROWS = 688
OUT_ROWS = 512
SUBLANES = 8
LANES = 128
PROJ_PIECES = 4
HALO = 8
WEIGHT_TILES = 8
VMEM_LIMIT = 56 * 1024 * 1024


def _cparams(n_axes):
    return pltpu.CompilerParams(
        dimension_semantics=("arbitrary",) * n_axes, vmem_limit_bytes=VMEM_LIMIT)


class _Layer:
    def __init__(self, stacked, layer):
        self.stacked, self.layer = stacked, layer
        self.shape = stacked.shape[1:]


def _operand(p):
    return p.stacked if isinstance(p, _Layer) else p


def _resident(p):
    zeros = (0,) * len(p.shape)
    if isinstance(p, _Layer):
        layer = p.layer
        return pl.BlockSpec((None,) + tuple(p.shape), lambda *_: (layer,) + zeros,
                            pipeline_mode=pl.Buffered(1))
    return pl.BlockSpec(p.shape, lambda *_: zeros, pipeline_mode=pl.Buffered(1))


def _rms(x, g):
    return x * lax.rsqrt(jnp.mean(x * x, axis=-1, keepdims=True) + RMS_EPS) * g


def _bf(x):
    return x.astype(BF16)


def _mm(a, b):
    return jnp.dot(_bf(a), _bf(b), preferred_element_type=F32)


def _mm_nt(a, b):
    return lax.dot_general(_bf(a), _bf(b), (((1,), (1,)), ((), ())), preferred_element_type=F32)


def _mm_tn(a, b):
    return lax.dot_general(_bf(a), _bf(b), (((0,), (0,)), ((), ())), preferred_element_type=F32)


def _softplus(x):
    return jnp.maximum(x, 0.0) + jnp.log(1.0 + jnp.exp(-jnp.abs(x)))


def _ffn_kernel(*refs, has_mix, embed_blocks):
    *refs, wi_ref, wo_ref = refs
    if has_mix:
        h_ref, ya_ref, yrw_ref, wm_ref, gm_ref, gpre_ref, wit_ref, wot_ref, gpost_ref, o_ref = refs
    elif embed_blocks:
        _, h_ref, meta_ref, gpre_ref, wit_ref, wot_ref, gpost_ref, o_ref = refs
    else:
        h_ref, gpre_ref, wit_ref, wot_ref, gpost_ref, o_ref = refs
    step = pl.program_id(0)

    @pl.when(step < WEIGHT_TILES)
    def _load_weights():
        rows_in, rows_out = wit_ref.shape[0], wot_ref.shape[0]
        wi_ref[pl.ds(pl.multiple_of(step * rows_in, rows_in), rows_in), :] = _bf(wit_ref[...])
        wo_ref[pl.ds(pl.multiple_of(step * rows_out, rows_out), rows_out), :] = _bf(wot_ref[...])
        o_ref[...] = h_ref[...]

    @pl.when(step >= WEIGHT_TILES)
    def _transform_block():
        h = h_ref[...]
        if embed_blocks:
            n_meta = meta_ref.shape[0]
            shifted = jnp.concatenate([meta_ref[...], h[0:h.shape[0] - n_meta]], axis=0)
            h = jnp.where(lax.rem(step - WEIGHT_TILES, embed_blocks) == 0, shifted, h)
        if has_mix:
            na = ya_ref.shape[1]
            m = jnp.dot(ya_ref[...], wm_ref[:na, :], preferred_element_type=F32)
            m = m + jnp.dot(yrw_ref[...], wm_ref[na:, :], preferred_element_type=F32)
            h = h + _rms(m, gm_ref[...])
        d_ff = wo_ref.shape[0]
        xn = _bf(_rms(h, gpre_ref[...]))
        gu = jnp.dot(xn, wi_ref[...], preferred_element_type=F32)
        act = _bf(jax.nn.silu(gu[:, :d_ff]) * gu[:, d_ff:])
        y = jnp.dot(act, wo_ref[...], preferred_element_type=F32)
        o_ref[...] = h + 0.5 * _rms(y, gpost_ref[...])


def _ffn(h, g_pre, w_in, w_out, g_post, blocks, mix=None, dense_out=False, embed=None):
    n, d = h.shape
    d_ff = w_out.shape[0]
    block_rows, n_blocks, first_row = blocks
    block_of = lambda step: jnp.maximum(step - WEIGHT_TILES, 0)
    tile_of = lambda step: jnp.minimum(step, WEIGHT_TILES - 1)

    def weight_tiles(w):
        layer, (n_rows, n_cols) = w.layer, w.shape
        return pl.BlockSpec((None, n_rows // WEIGHT_TILES, n_cols),
                            lambda i: (layer, tile_of(i), 0))

    element_rows = lambda width, start: pl.BlockSpec(
        (pl.Element(block_rows), pl.Element(width)), lambda i: (start(block_of(i)), 0))
    rows = lambda width: element_rows(width, first_row)
    operands, in_specs = [h], [rows(d)]
    if embed is not None:
        x, meta, _, x_first_row = embed
        operands += [x, meta]
        in_specs = [pl.BlockSpec(memory_space=pl.ANY), element_rows(d, x_first_row),
                    _resident(meta)]
    if mix is not None:
        y_a, y_rw, w_mix, g_mix = mix
        operands += [y_a, y_rw]
        in_specs += [rows(y_a.shape[1]), rows(y_rw.shape[1])]
        operands += [_operand(w_mix), _operand(g_mix)]
        in_specs += [_resident(w_mix), _resident(g_mix)]
    operands += [_operand(p) for p in (g_pre, w_in, w_out, g_post)]
    in_specs += [_resident(g_pre), weight_tiles(w_in), weight_tiles(w_out), _resident(g_post)]
    if dense_out:
        out_spec = pl.BlockSpec((block_rows, d), lambda i: (block_of(i), 0))
        out_shape, aliases = jax.ShapeDtypeStruct((n_blocks * block_rows, d), F32), {}
    else:
        out_spec, out_shape, aliases = rows(d), jax.ShapeDtypeStruct((n, d), F32), {0: 0}
    return pl.pallas_call(
        functools.partial(_ffn_kernel, has_mix=mix is not None,
                          embed_blocks=embed[2] if embed is not None else 0),
        grid=(WEIGHT_TILES + n_blocks,),
        in_specs=in_specs,
        out_specs=out_spec,
        out_shape=out_shape,
        scratch_shapes=[pltpu.VMEM(w_in.shape, BF16), pltpu.VMEM(w_out.shape, BF16)],
        input_output_aliases=aliases,
        compiler_params=_cparams(1),
        name="ffn_mix" if mix is not None else "ffn",
    )(*operands)


def _group_rms(x, g, ones_bd):
    ms = _mm(x * x, ones_bd) * (1.0 / HEAD)
    return x * lax.rsqrt(ms + RMS_EPS) * g


def _lrusc_stages(p, prm, ones_bd, xh_ref, ch_ref, hs_ref, o_ref):
    cw_ref, cb_ref, wa_ref, ba_ref, wx_ref, bx_ref, lam_ref, lg_ref, scw_ref, scg_ref = prm
    w = cb_ref.shape[1]
    rows = p.shape[0]
    xb = p[:, 0:w]
    gb = p[:, w:2 * w]
    sc_b = p[:, 2 * w:3 * w]
    cx = p[:, 3 * w:4 * w] * p[:, 4 * w:5 * w]

    xx = jnp.concatenate([xh_ref[...], xb], axis=0)
    u = xx[HALO - 3:HALO - 3 + rows] * cw_ref[0:1, :]
    for k in range(1, 4):
        u = u + xx[HALO - 3 + k:HALO - 3 + k + rows] * cw_ref[k:k + 1, :]
    u = u + cb_ref[...]
    xh_ref[...] = xb[rows - HALO:, :]

    yield
    r = jax.nn.sigmoid(_mm(u, wa_ref[...]) + ba_ref[...])
    gi = jax.nn.sigmoid(_mm(u, wx_ref[...]) + bx_ref[...])
    log_a = -LRU_C * r * _softplus(-lam_ref[...])
    a = jnp.exp(log_a)
    th = jnp.tanh(log_a)
    b = jnp.sqrt(-2.0 * th / (1.0 - th)) * (gi * u)

    sub = jnp.bitwise_and(lax.broadcasted_iota(jnp.int32, a.shape, 0), SUBLANES - 1)
    s = 1
    while s < SUBLANES:
        b = a * jnp.where(sub >= s, pltpu.roll(b, s, axis=0), 0.0) + b
        a = a * jnp.where(sub >= s, pltpu.roll(a, s, axis=0), 1.0)
        s *= 2
    carry = hs_ref[...]
    groups = []
    for r0 in range(0, rows, SUBLANES):
        hg = b[r0:r0 + SUBLANES] + a[r0:r0 + SUBLANES] * carry
        groups.append(hg)
        carry = hg[SUBLANES - 1:SUBLANES, :]
    hseq = jnp.concatenate(groups, axis=0)
    hs_ref[...] = carry
    lru_out = jax.nn.gelu(gb) * hseq

    cc = jnp.concatenate([ch_ref[...], cx], axis=0)
    conv = cc[HALO - 2:HALO - 2 + rows] * scw_ref[0:1, :]
    for k in range(1, 3):
        conv = conv + cc[HALO - 2 + k:HALO - 2 + k + rows] * scw_ref[k:k + 1, :]
    ch_ref[...] = cx[rows - HALO:, :]

    yield
    o_ref[:, 0:w] = _group_rms(lru_out, lg_ref[...], ones_bd).astype(o_ref.dtype)
    o_ref[:, w:2 * w] = _group_rms(sc_b * conv, scg_ref[...], ones_bd).astype(o_ref.dtype)


def _pair_bd(x):
    first = lax.broadcasted_iota(jnp.int32, x.shape, 1) < HEAD
    zero = jnp.zeros_like(x)
    return jnp.concatenate([jnp.where(first, x, zero), jnp.where(first, zero, x)], axis=0)


def _solve_unit_lower(lmats, rhss, same_blk, eye):
    dot = lambda a, b: jnp.dot(a, _pair_bd(b), preferred_element_type=F32)
    lds = [jnp.where(same_blk, l, 0.0) for l in lmats]
    los = [_bf(l - ld) for l, ld in zip(lmats, lds)]
    ts = [eye + ld for ld in lds]
    pbs = [_bf(ld) for ld in lds]
    n, lanes = lmats[0].shape
    pbs = [_bf(dot(pb, pb)) for pb in pbs]
    s = 2
    while s < INV_BLOCK:
        if 2 * s < INV_BLOCK:
            both = [dot(jnp.concatenate([pb, _bf(t)], axis=0), pb) for t, pb in zip(ts, pbs)]
            ts = [t + b[n:] for t, b in zip(ts, both)]
            pbs = [_bf(b[0:n]) for b in both]
        else:
            ts = [t + dot(_bf(t), pb) for t, pb in zip(ts, pbs)]
        s *= 2
    tbs = [_bf(t) for t in ts]

    def dot2(a, b1, b2):
        both = jnp.dot(a, jnp.concatenate([_pair_bd(b1), _pair_bd(b2)], axis=1),
                       preferred_element_type=F32)
        return both[:, 0:lanes], both[:, lanes:]

    both = [dot2(tb, _bf(r), lo) for tb, r, lo in zip(tbs, rhss, los)]
    xs = [b[0] for b in both]
    pbs = [_bf(b[1]) for b in both]
    nb = CHUNK // INV_BLOCK
    s = 1
    while s < nb:
        s *= 2
        if s < nb:
            both = [dot2(pb, _bf(x), pb) for x, pb in zip(xs, pbs)]
            xs = [x + b[0] for x, b in zip(xs, both)]
            pbs = [_bf(b[1]) for b in both]
        else:
            xs = [x + dot(pb, _bf(x)) for x, pb in zip(xs, pbs)]
    return xs


def _mixers_kernel(*refs, n_dl, n_il):
    (h_ref, gin_ref, win_ref), lru_prm, refs = refs[:3], refs[3:13], refs[13:]
    (mu_ref, w0_ref, a0_ref, wl_ref, kk_ref, ka_ref, rk_ref, lw_ref, lb_ref, bd_ref,
     ya_ref, o_ref, zl_ref, s_ref, p_ref, xh_ref, ch_ref, hs_ref) = refs
    width = w0_ref.shape[1]
    n_slots = h_ref.shape[0]
    n_chunks = h_ref.shape[1] // CHUNK
    n_rows = n_slots * CHUNK
    ones_bd = bd_ref[...]

    @pl.when(pl.program_id(1) == 0)
    def _():
        for ref in (zl_ref, s_ref, xh_ref, ch_ref, hs_ref):
            ref[...] = jnp.zeros_like(ref)

    t_rows = h_ref.shape[1]
    n_rw = p_ref.shape[2]
    n_a = win_ref.shape[1] - n_rw
    xn = _bf(_rms(h_ref[...].reshape(n_slots * t_rows, h_ref.shape[2]), gin_ref[...]))
    p_a = jnp.dot(xn, win_ref[:, 0:n_a], preferred_element_type=F32)
    pending = [_lrusc_stages(p_a[s * t_rows:(s + 1) * t_rows], lru_prm, ones_bd, xh_ref.at[s],
                             ch_ref.at[s], hs_ref.at[s], ya_ref.at[s]) for s in range(n_slots)]
    piece = -(-n_rw // (PROJ_PIECES * LANES)) * LANES
    for c0 in range(0, n_rw, piece):
        c1 = min(c0 + piece, n_rw)
        p_rw = jnp.dot(xn, win_ref[:, n_a + c0:n_a + c1], preferred_element_type=F32)
        first_row = lax.broadcasted_iota(jnp.int32, (SUBLANES, c1 - c0), 0) < 1
        for s in range(n_slots):
            z = p_rw[s * t_rows:(s + 1) * t_rows]
            rolled = pltpu.roll(z, 1, axis=0)
            zprev = jnp.concatenate([jnp.where(first_row, zl_ref[s, :, c0:c1], rolled[0:SUBLANES]),
                                     rolled[SUBLANES:]], axis=0)
            zl_ref[s, :, c0:c1] = z[t_rows - 1:t_rows, :]
            p_ref[s, :, c0:c1] = z + (zprev - z) * mu_ref[:, c0:c1]
        pending = [g for g in pending if next(g, True) is None]
    while pending:
        pending = [g for g in pending if next(g, True) is None]

    pair = 2 * HEAD
    n_pairs = width // pair
    c2 = 2 * CHUNK
    ri = lax.broadcasted_iota(jnp.int32, (c2, 2 * pair), 0)
    cm = jnp.bitwise_and(lax.broadcasted_iota(jnp.int32, (c2, 2 * pair), 1), CHUNK - 1)
    gmask = ((ri < CHUNK) & (cm < ri)) | ((ri >= CHUNK) & (cm <= ri - CHUNK))
    ti = lax.broadcasted_iota(jnp.int32, (CHUNK, pair), 0)
    si = jnp.bitwise_and(lax.broadcasted_iota(jnp.int32, (CHUNK, pair), 1), CHUNK - 1)
    shift = INV_BLOCK.bit_length() - 1
    same_blk = jnp.right_shift(ti, shift) == jnp.right_shift(si, shift)
    eye = (ti == si).astype(F32)
    ti = lax.broadcasted_iota(jnp.int32, (CHUNK, CHUNK), 0)
    tri = (lax.broadcasted_iota(jnp.int32, (CHUNK, CHUNK), 1) <= ti).astype(BF16)
    same_head = ((lax.broadcasted_iota(jnp.int32, (pair, pair), 0) < HEAD)
                 == (lax.broadcasted_iota(jnp.int32, (pair, pair), 1) < HEAD))
    bd_w = ones_bd.shape[0]

    def seg_sum(x):
        xb = _bf(x)
        parts = [jnp.dot(xb[:, c:c + bd_w], ones_bd, preferred_element_type=F32)
                 for c in range(0, x.shape[1], bd_w)]
        return jnp.concatenate(parts, axis=1)

    slots = range(n_slots)
    stack = lambda xs: jnp.concatenate(xs, axis=0)
    unstack = lambda x: [x[s * CHUNK:(s + 1) * CHUNK] for s in slots]

    def chunk_body(c, carry):
        r0 = pl.multiple_of(c * CHUNK, CHUNK)
        zz = stack([p_ref[s, pl.ds(r0, CHUNK), :] for s in slots])
        r = zz[:, 0:width]
        k = zz[:, width:2 * width]
        v = zz[:, 2 * width:3 * width]
        lo = zz[:, 3 * width:]
        lane = lax.broadcasted_iota(jnp.int32, lo.shape, 1)
        lo_act = jnp.where(lane < n_dl, jnp.tanh(lo),
                           jnp.where(lane < n_dl + n_il, lo, jax.nn.sigmoid(lo)))
        lora = _mm(lo_act, wl_ref[...])
        lw = -DECAY_SCALE * jax.nn.sigmoid(w0_ref[...] + lora[:, 0:width])
        a = jax.nn.sigmoid(a0_ref[...] + lora[:, width:2 * width])
        g = lora[:, 2 * width:3 * width]
        kk = k * kk_ref[...]
        kf = k * (1.0 + (a - 1.0) * ka_ref[...])
        sums = seg_sum(jnp.concatenate([kk * kk, r * kf * rk_ref[...]], axis=0))
        kk = kk * lax.rsqrt(jnp.maximum(sums[0:n_rows], 1e-24))
        bonus = sums[n_rows:]
        kka = kk * a

        lw_hi = _bf(lw)
        lw_lo = _bf(lw - lw_hi.astype(F32))
        hi_l = jnp.concatenate(unstack(lw_hi), axis=1)
        lo_l = jnp.concatenate(unstack(lw_lo), axis=1)
        cum_l = (jnp.dot(tri, hi_l, preferred_element_type=F32)
                 + jnp.dot(tri, lo_l, preferred_element_type=F32))
        cum = stack([cum_l[:, s * width:(s + 1) * width] for s in slots])
        w_c = jnp.exp(cum_l[CHUNK - 1:CHUNK, :])
        w_t = jnp.exp(cum)
        w_in = 1.0 / w_t
        w_rem = stack([w * w_c[:, s * width:(s + 1) * width]
                       for s, w in enumerate(unstack(w_in))])
        al = unstack(_bf(-kk * jnp.exp(cum - lw)))
        be = unstack(_bf(kka * w_in))
        kt = unstack(_bf(kf * w_in))
        rt = unstack(_bf(r * w_t))
        bw = unstack(_bf(kka * w_rem))
        kw = unstack(_bf(kf * w_rem))
        vb = unstack(_bf(v))

        chains = [(s, slice(j * pair, (j + 1) * pair)) for s in slots for j in range(n_pairs)]
        cat = jnp.concatenate
        ars = [cat([al[s][:, ps], rt[s][:, ps]], axis=0) for s, ps in chains]
        bks = [cat([_pair_bd(be[s][:, ps]), _pair_bd(kt[s][:, ps])], axis=0) for s, ps in chains]
        bkws = [cat([bw[s][:, ps], kw[s][:, ps]], axis=0) for s, ps in chains]
        vps = [vb[s][:, ps] for s, ps in chains]
        vbds = [_pair_bd(v_p) for v_p in vps]
        wcs = [w_c[:, s * width + j * pair:s * width + (j + 1) * pair]
               for s in slots for j in range(n_pairs)]
        n_chains = n_slots * n_pairs
        s0s = [s_ref[i] for i in range(n_chains)]
        gas = [_mm_nt(ar, cat([bk, _bf(s0)], axis=0)) for ar, bk, s0 in zip(ars, bks, s0s)]
        gms = [jnp.where(gmask, ga[:, 0:2 * pair], 0.0) for ga in gas]
        a_ss = [ga[:, 2 * pair:] for ga in gas]
        akvs = [_mm(gm[0:CHUNK, pair:], vbd) for gm, vbd in zip(gms, vbds)]
        rhss = [a_s[0:CHUNK] + akv for a_s, akv in zip(a_ss, akvs)]
        es = _solve_unit_lower([gm[0:CHUNK, 0:pair] for gm in gms], rhss, same_blk, eye)
        ebs = [_bf(e) for e in es]
        ys = [a_s[CHUNK:] + _mm(gm[CHUNK:, :], cat([_pair_bd(eb), vbd], axis=0))
              for a_s, gm, eb, vbd in zip(a_ss, gms, ebs, vbds)]
        for i in range(n_chains):
            upd = _mm_tn(cat([ebs[i], vps[i]], axis=0), bkws[i])
            s_ref[i] = s0s[i] * wcs[i] + jnp.where(same_head, upd, 0.0)

        y = stack([cat(ys[s * n_pairs:(s + 1) * n_pairs], axis=1) for s in slots])
        mean = seg_sum(y) * (1.0 / HEAD)
        d = y - mean
        var = seg_sum(d * d) * (1.0 / HEAD)
        yn = d * lax.rsqrt(var + LNX_EPS) * lw_ref[...] + lb_ref[...]
        out = unstack(((yn + bonus * v) * g).astype(o_ref.dtype))
        for s in slots:
            o_ref[s, pl.ds(r0, CHUNK), :] = out[s]
        return carry

    lax.fori_loop(0, n_chunks, chunk_body, 0)


def _mixers(h, g_in, w_in, lru_params, rw_params, batch, ones_bd, n_dl, n_il):
    n, d = h.shape
    nrw = rw_params[0].shape[1]
    width = rw_params[1].shape[1]
    w_a = lru_params[1].shape[1]
    t_pad = n // batch
    blk = lambda b, t: (b, t, 0)
    params = (g_in, w_in) + tuple(lru_params) + tuple(rw_params) + (ones_bd,)
    out_blk = lambda w: pl.BlockSpec((RWKV_SLOTS, RWKV_ROWS, w), blk)
    y_a, y_rw = pl.pallas_call(
        functools.partial(_mixers_kernel, n_dl=n_dl, n_il=n_il),
        grid=(batch // RWKV_SLOTS, t_pad // RWKV_ROWS),
        in_specs=[pl.BlockSpec((RWKV_SLOTS, RWKV_ROWS, d), blk)] + [_resident(p) for p in params],
        out_specs=[out_blk(2 * w_a), out_blk(width)],
        out_shape=[jax.ShapeDtypeStruct((batch, t_pad, 2 * w_a), BF16),
                   jax.ShapeDtypeStruct((batch, t_pad, width), BF16)],
        scratch_shapes=[pltpu.VMEM((RWKV_SLOTS, 1, nrw), F32),
                        pltpu.VMEM((RWKV_SLOTS * (width // (2 * HEAD)), 2 * HEAD, 2 * HEAD), F32),
                        pltpu.VMEM((RWKV_SLOTS, RWKV_ROWS, nrw), F32),
                        pltpu.VMEM((RWKV_SLOTS, HALO, w_a), F32),
                        pltpu.VMEM((RWKV_SLOTS, HALO, w_a), F32),
                        pltpu.VMEM((RWKV_SLOTS, 1, w_a), F32)],
        compiler_params=_cparams(2),
        name="mixers",
    )(h.reshape(batch, t_pad, d), *[_operand(p) for p in params])
    return y_a.reshape(n, 2 * w_a), y_rw.reshape(n, width)


def _ones_block_diag(width):
    return jnp.asarray(np.kron(np.eye(width // HEAD), np.ones((HEAD, HEAD))), BF16)


def _block_diag(w):
    n, g, i, j = w.shape
    return jnp.einsum('lgij,gh->lgihj', w, jnp.eye(g, dtype=w.dtype)).reshape(n, g * i, g * j)


def kernel(x, meta_tokens, norm_g, ffn1_w_in, ffn1_w_out, ffn2_w_in, ffn2_w_out, mix_w_in, mix_w_out, lru_conv_w, lru_conv_b, lru_wa, lru_ba, lru_wx, lru_bx, lru_lambda, lru_norm_g, sc_conv_w, sc_norm_g, rwkv_mu, rwkv_w0, rwkv_w2, rwkv_a0, rwkv_a2, rwkv_g2, rwkv_k_k, rwkv_k_a, rwkv_r_k, rwkv_lnx_w, rwkv_lnx_b):
    bsz, seq, d = x.shape
    n_meta = meta_tokens.shape[0]
    depth = norm_g.shape[0]
    lru_w = lru_conv_b.shape[1]
    sc_w = sc_norm_g.shape[1]
    rw_w = rwkv_w0.shape[1]
    t_real = n_meta + seq
    t_pad = -(-t_real // RWKV_ROWS) * RWKV_ROWS
    assert lru_w == sc_w and bsz % RWKV_SLOTS == 0
    assert t_real % FFN_ROWS == 0 and seq % OUT_ROWS == 0

    h = jnp.zeros((bsz * t_pad, d), x.dtype)

    ones_bd = _ones_block_diag(lru_w)
    rows3 = lambda p: p.reshape(-1, 1, p.shape[-1])
    f1_in, f1_out, f2_in, f2_out = ffn1_w_in, ffn1_w_out, ffn2_w_in, ffn2_w_out
    assert d % (2 * SUBLANES * WEIGHT_TILES) == 0 and f1_out.shape[1] % (2 * SUBLANES * WEIGHT_TILES) == 0
    m_in, m_out = _bf(mix_w_in), _bf(mix_w_out)
    wa_bd, wx_bd = _bf(_block_diag(lru_wa)), _bf(_block_diag(lru_wx))
    n_dl, n_il, n_gl = rwkv_w2.shape[1], rwkv_a2.shape[1], rwkv_g2.shape[1]
    lora_w = jnp.zeros((depth, n_dl + n_il + n_gl, 3 * rw_w), F32)
    lora_w = lora_w.at[:, 0:n_dl, 0:rw_w].set(rwkv_w2)
    lora_w = lora_w.at[:, n_dl:n_dl + n_il, rw_w:2 * rw_w].set(rwkv_a2)
    lora_w = _bf(lora_w.at[:, n_dl + n_il:, 2 * rw_w:].set(rwkv_g2))
    norm_rows = rows3(norm_g)
    n_norms = norm_g.shape[1]
    lru_stacked = (lru_conv_w, rows3(lru_conv_b), wa_bd, rows3(lru_ba), wx_bd, rows3(lru_bx),
                   rows3(lru_lambda), rows3(lru_norm_g), sc_conv_w, rows3(sc_norm_g))
    rw_stacked = (rows3(rwkv_mu), rows3(rwkv_w0), rows3(rwkv_a0), lora_w, rows3(rwkv_k_k),
                  rows3(rwkv_k_a), rows3(rwkv_r_k), rows3(rwkv_lnx_w), rows3(rwkv_lnx_b))

    def row_blocks(block_rows, first, count):
        align = math.gcd(t_pad, first, block_rows)
        per_seq = count // block_rows
        return (block_rows, bsz * per_seq, lambda i: pl.multiple_of(
            (i // per_seq) * t_pad + first + (i % per_seq) * block_rows, align))

    real_rows = row_blocks(FFN_ROWS, 0, t_real)
    out_rows = row_blocks(OUT_ROWS, n_meta, seq)
    per_seq = t_real // FFN_ROWS
    x_align = math.gcd(seq, n_meta, FFN_ROWS)
    embed = (x.reshape(bsz * seq, d), meta_tokens.astype(x.dtype), per_seq,
             lambda i: pl.multiple_of((i // per_seq) * seq
                                      + jnp.maximum((i % per_seq) * FFN_ROWS - n_meta, 0), x_align))

    for l in range(depth):
        g = [_Layer(norm_rows, l * n_norms + i) for i in range(n_norms)]
        h = _ffn(h, g[0], _Layer(f1_in, l), _Layer(f1_out, l), g[1], real_rows,
                 embed=embed if l == 0 else None)
        y_a, y_rw = _mixers(h, g[2], _Layer(m_in, l), [_Layer(p, l) for p in lru_stacked],
                            [_Layer(p, l) for p in rw_stacked], bsz, ones_bd, n_dl, n_il)
        last = l == depth - 1
        h = _ffn(h, g[4], _Layer(f2_in, l), _Layer(f2_out, l), g[5],
                 out_rows if last else real_rows,
                 mix=(y_a, y_rw, _Layer(m_out, l), g[3]), dense_out=last)

    return h.reshape(bsz, seq, d)
```
